```python
import jax
import jax.numpy as jnp
from jax import lax
import numpy as np

D_MODEL = 2048
BATCH = 2
SEQ = 16384
DEPTH = 1

CHUNK = 64
MIX_WIDTH = D_MODEL
GMLP_WIDTH = MIX_WIDTH // 2
GMLP_HEAD_DIM = 128
GMLP_HEADS = GMLP_WIDTH // GMLP_HEAD_DIM
GMLP_BLOCK = 128
POOL_WIDTH = MIX_WIDTH - GMLP_WIDTH
POOL_WINDOWS = (2, 4, 8, 16)
POOL_GROUPS = len(POOL_WINDOWS)
POOL_GROUP_DIM = POOL_WIDTH // POOL_GROUPS
IN_PROJ_WIDTH = 2 * GMLP_WIDTH + POOL_WIDTH
N_EXPERTS = 32
TOP_K = 4
EXPERT_DIM = D_MODEL
SWIGLU_ALPHA = 1.702
SWIGLU_LIMIT = 7.0
MOE_BLOCK = 256
RMS_EPS = 1e-5
LN_EPS = 1e-5

kernel_name = 'hybrid_gmlp_pool_moe_adaln_block'


def rms_norm(x, g):
    xf = x.astype(jnp.float32)
    y = xf * lax.rsqrt(jnp.mean(xf * xf, axis=-1, keepdims=True) + RMS_EPS)
    return (y * g.astype(jnp.float32)).astype(x.dtype)


def gmlp_spatial_gating(u, v, ln_g, ln_b, w_s, b_s):
    bsz, seq, _ = u.shape
    n_blk = seq // GMLP_BLOCK
    vh = v.astype(jnp.float32).reshape(bsz, seq, GMLP_HEADS, GMLP_HEAD_DIM)
    mu = jnp.mean(vh, axis=-1, keepdims=True)
    var = jnp.mean(jnp.square(vh - mu), axis=-1, keepdims=True)
    vn = (vh - mu) * lax.rsqrt(var + LN_EPS)
    vn = vn * ln_g.astype(jnp.float32).reshape(GMLP_HEADS, GMLP_HEAD_DIM) + ln_b.astype(jnp.float32).reshape(GMLP_HEADS, GMLP_HEAD_DIM)
    vn = vn.astype(u.dtype).reshape(bsz, n_blk, GMLP_BLOCK, GMLP_HEADS, GMLP_HEAD_DIM)
    chunk_id = jnp.arange(GMLP_BLOCK) // CHUNK
    mask = chunk_id[:, None] >= chunk_id[None, :]
    ws = jnp.where(mask[None], w_s, jnp.zeros((), w_s.dtype))
    mixed = jnp.einsum('hij,bnjhd->bnihd', ws, vn) + b_s.T[None, None, :, :, None]
    return u * mixed.reshape(bsz, seq, GMLP_WIDTH)


def multiscale_pool(p, w_pool, scale):
    bsz, seq, _ = p.shape
    pf = p.astype(jnp.float32).reshape(bsz, seq, POOL_GROUPS, POOL_GROUP_DIM)
    cs = jnp.cumsum(pf, axis=1)
    t1 = jnp.arange(1, seq + 1, dtype=jnp.float32)
    outs = []
    for g, w in enumerate(POOL_WINDOWS):
        csg = cs[:, :, g]
        lag = jnp.pad(csg, ((0, 0), (w, 0), (0, 0)))[:, :seq]
        cnt = jnp.minimum(t1, float(w))[None, :, None]
        outs.append((csg - lag) / cnt - pf[:, :, g])
    pooled = jnp.stack(outs, axis=2).astype(p.dtype)
    y = jnp.einsum('bsgc,gcd->bsgd', pooled, w_pool)
    return y.reshape(bsz, seq, POOL_WIDTH) * scale


def moe_ffn(h, router_w, router_b, w1, b1, w2, b2):
    bsz, seq, d = h.shape
    n_tok = bsz * seq
    hf = h.reshape(n_tok, d)
    logits = hf.astype(jnp.float32) @ router_w.astype(jnp.float32) + router_b.astype(jnp.float32)
    top_val, top_idx = lax.top_k(logits, TOP_K)
    gates = jax.nn.softmax(top_val, axis=-1)
    n_asg = n_tok * TOP_K
    flat_e = top_idx.reshape(n_asg).astype(jnp.int32)
    order = jnp.argsort(flat_e)
    sorted_e = flat_e[order]
    counts = jnp.bincount(flat_e, length=N_EXPERTS).astype(jnp.int32)
    padded = (counts + MOE_BLOCK - 1) // MOE_BLOCK * MOE_BLOCK
    pad_end = jnp.cumsum(padded)
    pad_start = pad_end - padded
    grp_start = jnp.cumsum(counts) - counts
    dest_sorted = pad_start[sorted_e] + (jnp.arange(n_asg, dtype=jnp.int32) - grp_start[sorted_e])
    dest = jnp.zeros((n_asg,), jnp.int32).at[order].set(dest_sorted)
    n_rows = (n_asg + MOE_BLOCK - 1) // MOE_BLOCK * MOE_BLOCK + N_EXPERTS * MOE_BLOCK
    n_blocks = n_rows // MOE_BLOCK
    rows = jnp.zeros((n_rows, d), h.dtype).at[dest].set(jnp.repeat(hf, TOP_K, axis=0))
    block_start = jnp.arange(n_blocks, dtype=jnp.int32) * MOE_BLOCK
    block_e = jnp.minimum(jnp.searchsorted(pad_end, block_start, side='right'), N_EXPERTS - 1).astype(jnp.int32)

    def expert_block(args):
        xb, e = args
        a = xb @ w1[e] + b1[e]
        glu = jnp.minimum(a[:, :EXPERT_DIM], SWIGLU_LIMIT)
        lin = jnp.clip(a[:, EXPERT_DIM:], -SWIGLU_LIMIT, SWIGLU_LIMIT)
        act = glu * jax.nn.sigmoid(SWIGLU_ALPHA * glu) * (lin + 1.0)
        return act @ w2[e] + b2[e]

    out_rows = lax.map(expert_block, (rows.reshape(n_blocks, MOE_BLOCK, d), block_e)).reshape(n_rows, d)
    out = out_rows[dest].reshape(n_tok, TOP_K, d)
    y = jnp.einsum('tk,tkd->td', gates.astype(h.dtype), out)
    return y.reshape(bsz, seq, d)


def setup_inputs(seed: int = 0) -> dict:
    key = jax.random.key(seed)
    ks = jax.random.split(key, 24)
    f32 = jnp.float32
    L, D, F, E = DEPTH, D_MODEL, EXPERT_DIM, N_EXPERTS

    def nrm(k, shape, s):
        return jax.random.normal(k, shape, f32) * s

    return {
        'x': nrm(ks[0], (BATCH, SEQ, D), 1.0),
        'c': nrm(ks[1], (BATCH, D), 1.0),
        'mix_norm_g': 1.0 + nrm(ks[2], (L, D), 0.02),
        'w_ada': nrm(ks[3], (L, D, 6 * D), D ** -0.5),
        'b_ada': nrm(ks[4], (L, 6 * D), 0.02),
        'w_in': nrm(ks[5], (L, D, IN_PROJ_WIDTH), D ** -0.5),
        'gmlp_ln_g': 1.0 + nrm(ks[6], (L, GMLP_WIDTH), 0.02),
        'gmlp_ln_b': nrm(ks[7], (L, GMLP_WIDTH), 0.02),
        'gmlp_ws': nrm(ks[8], (L, GMLP_HEADS, GMLP_BLOCK, GMLP_BLOCK), GMLP_BLOCK ** -0.5),
        'gmlp_bs': 1.0 + nrm(ks[9], (L, GMLP_HEADS, GMLP_BLOCK), 0.1),
        'pool_w': nrm(ks[10], (L, POOL_GROUPS, POOL_GROUP_DIM, POOL_GROUP_DIM), POOL_GROUP_DIM ** -0.5),
        'pool_scale': 1.0 + nrm(ks[11], (L, POOL_WIDTH), 0.1),
        'gmlp_out_g': 1.0 + nrm(ks[12], (L, GMLP_WIDTH), 0.02),
        'pool_out_g': 1.0 + nrm(ks[13], (L, POOL_WIDTH), 0.02),
        'w_out': nrm(ks[14], (L, MIX_WIDTH, D), MIX_WIDTH ** -0.5),
        'ffn_norm_g': 1.0 + nrm(ks[15], (L, D), 0.02),
        'router_w': nrm(ks[16], (L, D, E), D ** -0.5),
        'router_b': nrm(ks[17], (L, E), 0.01),
        'moe_w1': nrm(ks[18], (L, E, D, 2 * F), D ** -0.5),
        'moe_b1': nrm(ks[19], (L, E, 2 * F), 0.02),
        'moe_w2': nrm(ks[20], (L, E, F, D), F ** -0.5),
        'moe_b2': nrm(ks[21], (L, E, D), 0.02),
        'final_norm_g': 1.0 + nrm(ks[22], (D,), 0.02),
    }


def reference(x, c, mix_norm_g, w_ada, b_ada, w_in, gmlp_ln_g, gmlp_ln_b, gmlp_ws, gmlp_bs, pool_w, pool_scale, gmlp_out_g, pool_out_g, w_out, ffn_norm_g, router_w, router_b, moe_w1, moe_b1, moe_w2, moe_b2, final_norm_g):
    cond = jax.nn.silu(c)
    for l in range(DEPTH):
        mod = (cond @ w_ada[l] + b_ada[l])[:, None, :]
        shift_m, scale_m, gate_m, shift_f, scale_f, gate_f = jnp.split(mod, 6, axis=-1)
        h = rms_norm(x, mix_norm_g[l]) * (1.0 + scale_m) + shift_m
        z = h @ w_in[l]
        u = jax.nn.gelu(z[..., :GMLP_WIDTH], approximate=False)
        v = jax.nn.gelu(z[..., GMLP_WIDTH:2 * GMLP_WIDTH], approximate=False)
        p = z[..., 2 * GMLP_WIDTH:]
        a_out = gmlp_spatial_gating(u, v, gmlp_ln_g[l], gmlp_ln_b[l], gmlp_ws[l], gmlp_bs[l])
        b_out = multiscale_pool(p, pool_w[l], pool_scale[l])
        mixed = jnp.concatenate([rms_norm(a_out, gmlp_out_g[l]), rms_norm(b_out, pool_out_g[l])], axis=-1)
        x = x + gate_m * (mixed @ w_out[l])
        h = rms_norm(x, ffn_norm_g[l]) * (1.0 + scale_f) + shift_f
        x = x + gate_f * moe_ffn(h, router_w[l], router_b[l], moe_w1[l], moe_b1[l], moe_w2[l], moe_b2[l])
    return rms_norm(x, final_norm_g)
```

```python
import functools

import jax
import jax.numpy as jnp
from jax import lax
from jax.experimental import pallas as pl
from jax.experimental.pallas import tpu as pltpu

F32 = jnp.float32
BF16 = jnp.bfloat16
I32 = jnp.int32
U32 = jnp.uint32

D_MODEL = 2048
GMLP_WIDTH = 1024
HEAD_DIM = 128
N_HEADS = GMLP_WIDTH // HEAD_DIM
GMLP_BLOCK = 128
CHUNK = 64
POOL_WIDTH = 1024
POOL_WINDOWS = (2, 4, 8, 16)
POOL_GROUP_DIM = POOL_WIDTH // len(POOL_WINDOWS)
POOL_HALO = 16
N_EXPERTS = 32
TOP_K = 4
EXPERT_DIM = 2048
SWIGLU_ALPHA = 1.702
SWIGLU_LIMIT = 7.0
EPS = 1e-5

LANES = 128
SUB_ROWS = 256
TILE_SUBS = 4
TILE_ROWS = SUB_ROWS * TILE_SUBS
F_CHUNK = 256
N_FCHUNKS = EXPERT_DIM // F_CHUNK
HALF = D_MODEL // 2

MIX_ROWS = 256
ROUTE_ROWS = 512
DISPATCH_ROWS = 512
COMBINE_ROWS = 256
ADA_COLS = 1024

VMEM_LIMIT = 56 * 1024 * 1024


def _params(n_axes, vmem=VMEM_LIMIT):
    return pltpu.CompilerParams(dimension_semantics=("arbitrary",) * n_axes, vmem_limit_bytes=vmem)


def _rms(x, g):
    return x * lax.rsqrt(jnp.mean(x * x, axis=-1, keepdims=True) + EPS) * g


def _gelu(x):
    return 0.5 * x * (1.0 + lax.erf(x * (2.0 ** -0.5)))


def _ada_body(c_ref, w_ref, b_ref, o_ref):
    c = c_ref[...]
    cond = c * jax.nn.sigmoid(c)
    o_ref[...] = jnp.dot(cond.astype(BF16), w_ref[...].astype(BF16), preferred_element_type=F32) + b_ref[...]


def _ada(c_pad, w_ada, b_ada):
    rows, d = c_pad.shape
    n = w_ada.shape[1]
    return pl.pallas_call(
        _ada_body,
        grid=(n // ADA_COLS,),
        in_specs=[pl.BlockSpec((rows, d), lambda j: (0, 0)),
                  pl.BlockSpec((d, ADA_COLS), lambda j: (0, j)),
                  pl.BlockSpec((1, ADA_COLS), lambda j: (0, j))],
        out_specs=pl.BlockSpec((rows, ADA_COLS), lambda j: (0, j)),
        out_shape=jax.ShapeDtypeStruct((rows, n), F32),
        compiler_params=_params(1),
        name="ada",
    )(c_pad, w_ada, b_ada)


def _pack_bf16_pair(lo, hi):
    lo_bits = pltpu.bitcast(lo.astype(BF16).astype(F32), U32)
    hi_bits = pltpu.bitcast(hi.astype(BF16).astype(F32), U32)
    return hi_bits | (lo_bits >> 16)


def _unpack_bf16_pair(p):
    lo = pltpu.bitcast(p << 16, F32).astype(BF16)
    hi = pltpu.bitcast(p & jnp.uint32(0xFFFF0000), F32).astype(BF16)
    return lo, hi


def _mix_body(x_ref, mod_ref, mixg_ref, win_ref, lng_ref, lnb_ref, ws_ref, bs_ref, pw_ref, pscale_ref,
              gog_ref, pog_ref, wout_ref, ffng_ref, rw_ref, rb_ref,
              x1_ref, h2p_ref, logit_ref,
              pe_ref, ab_ref, cat_ref, *, tiles_per_seq):
    tr = x_ref.shape[0]
    d = D_MODEL
    seq_tile = pl.program_id(0) % tiles_per_seq
    x = x_ref[...]
    shift_m = mod_ref[:, 0 * d:1 * d]
    scale_m = mod_ref[:, 1 * d:2 * d]
    gate_m = mod_ref[:, 2 * d:3 * d]
    shift_f = mod_ref[:, 3 * d:4 * d]
    scale_f = mod_ref[:, 4 * d:5 * d]

    hb = (_rms(x, mixg_ref[...]) * (1.0 + scale_m) + shift_m).astype(BF16)

    u = _gelu(jnp.dot(hb, win_ref[:, 0:GMLP_WIDTH], preferred_element_type=F32))
    v = _gelu(jnp.dot(hb, win_ref[:, GMLP_WIDTH:2 * GMLP_WIDTH], preferred_element_type=F32))
    slab = 2 * GMLP_BLOCK
    ri = lax.broadcasted_iota(I32, (slab, slab), 0)
    ci = lax.broadcasted_iota(I32, (slab, slab), 1)
    same_block = (ri // GMLP_BLOCK) == (ci // GMLP_BLOCK)
    causal = ((ri % GMLP_BLOCK) // CHUNK) >= ((ci % GMLP_BLOCK) // CHUNK)
    keep = same_block & causal
    ssq_a = jnp.zeros((tr, 1), F32)
    for h in range(N_HEADS):
        sl = slice(h * HEAD_DIM, (h + 1) * HEAD_DIM)
        vh = v[:, sl]
        dv = vh - jnp.mean(vh, axis=-1, keepdims=True)
        var = jnp.mean(dv * dv, axis=-1, keepdims=True)
        vn = (dv * lax.rsqrt(var + EPS) * lng_ref[:, sl] + lnb_ref[:, sl]).astype(BF16)
        w_sp = jnp.where(keep, ws_ref[h], 0.0).astype(BF16)
        for s in range(tr // slab):
            rows = slice(s * slab, (s + 1) * slab)
            mixed = jnp.dot(w_sp, vn[rows], preferred_element_type=F32) + bs_ref[h]
            a = u[rows, sl] * mixed
            ab_ref[rows, sl] = a
    a_all = ab_ref[...]
    ssq_a = jnp.sum(a_all * a_all, axis=-1, keepdims=True)
    cat_ref[:, 0:GMLP_WIDTH] = (a_all * lax.rsqrt(ssq_a / GMLP_WIDTH + EPS) * gog_ref[...]).astype(BF16)

    p = jnp.dot(hb, win_ref[:, 2 * GMLP_WIDTH:], preferred_element_type=F32)

    @pl.when(seq_tile == 0)
    def _():
        pe_ref[0:POOL_HALO, :] = jnp.zeros((POOL_HALO, POOL_WIDTH), F32)

    pe_ref[POOL_HALO:, :] = p
    pos1 = (seq_tile * tr + lax.broadcasted_iota(I32, (tr, 1), 0) + 1).astype(F32)
    for g, w in enumerate(POOL_WINDOWS):
        cs = slice(g * POOL_GROUP_DIM, (g + 1) * POOL_GROUP_DIM)
        e = pe_ref[:, cs]
        s = e
        shift = 1
        while shift < w:
            s = s + pltpu.roll(s, shift, 0)
            shift *= 2
        inv = 1.0 / jnp.minimum(pos1, float(w))
        pooled = s[POOL_HALO:] * inv - e[POOL_HALO:]
        y = jnp.dot(pooled.astype(BF16), pw_ref[g], preferred_element_type=F32) * pscale_ref[:, cs]
        ab_ref[:, cs] = y
    pe_ref[0:POOL_HALO, :] = pe_ref[tr:tr + POOL_HALO, :]
    b_all = ab_ref[...]
    ssq_b = jnp.sum(b_all * b_all, axis=-1, keepdims=True)
    cat_ref[:, GMLP_WIDTH:] = (b_all * lax.rsqrt(ssq_b / POOL_WIDTH + EPS) * pog_ref[...]).astype(BF16)

    x1 = x + gate_m * jnp.dot(cat_ref[...], wout_ref[...], preferred_element_type=F32)
    x1_ref[...] = x1
    h2 = _rms(x1, ffng_ref[...]) * (1.0 + scale_f) + shift_f
    h2p_ref[...] = _pack_bf16_pair(h2[:, :HALF], h2[:, HALF:])
    h_hi = h2.astype(BF16)
    h_lo = (h2 - h_hi.astype(F32)).astype(BF16)
    l_hi = jnp.dot(h_hi, rw_ref[...], preferred_element_type=F32)
    l_lo = jnp.dot(h_lo, rw_ref[:, 0:LANES], preferred_element_type=F32)
    logit_ref[...] = l_hi[:, 0:LANES] + l_hi[:, LANES:] + l_lo + rb_ref[...]


def _const_spec(shape):
    nd = len(shape)
    return pl.BlockSpec(shape, lambda i: (0,) * nd, pipeline_mode=pl.Buffered(1))


def _mix(x2d, mod3, seq, mix_g, w_in, ln_g, ln_b, ws2, bs2, pool_w, pool_scale, go_g, po_g, w_out, ffn_g, rw, rb):
    t, d = x2d.shape
    tr = MIX_ROWS
    tps = seq // tr
    row_spec = lambda cols: pl.BlockSpec((tr, cols), lambda i: (i, 0))
    consts = [mix_g, w_in, ln_g, ln_b, ws2, bs2, pool_w, pool_scale, go_g, po_g, w_out, ffn_g, rw, rb]
    return pl.pallas_call(
        functools.partial(_mix_body, tiles_per_seq=tps),
        grid=(t // tr,),
        in_specs=[row_spec(d),
                  pl.BlockSpec((None, 1, mod3.shape[2]), lambda i: (i // tps, 0, 0))]
                 + [_const_spec(a.shape) for a in consts],
        out_specs=[row_spec(d), row_spec(HALF), row_spec(LANES)],
        out_shape=[jax.ShapeDtypeStruct((t, d), F32),
                   jax.ShapeDtypeStruct((t, HALF), U32),
                   jax.ShapeDtypeStruct((t, LANES), F32)],
        scratch_shapes=[pltpu.VMEM((tr + POOL_HALO, POOL_WIDTH), F32),
                        pltpu.VMEM((tr, GMLP_WIDTH), F32),
                        pltpu.VMEM((tr, d), BF16)],
        compiler_params=_params(1),
        name="mix",
    )(x2d, mod3, *consts)


def _route_body(lg_ref, dest_ref, gate_ref, cnt_ref, tot_ref, run_ref, start_ref):
    phase = pl.program_id(0)
    i = pl.program_id(1)
    tt = lg_ref.shape[0]
    lane = lax.broadcasted_iota(I32, (tt, LANES), 1)
    l = jnp.where(lane < N_EXPERTS, lg_ref[...], -jnp.inf)
    sels, vals = [], []
    for _ in range(TOP_K):
        m = jnp.max(l, axis=1, keepdims=True)
        idx = jnp.min(jnp.where(l == m, lane, LANES), axis=1, keepdims=True)
        sel = lane == idx
        sels.append(sel)
        vals.append(m)
        l = jnp.where(sel, -jnp.inf, l)
    onehot = sels[0].astype(F32)
    for sel in sels[1:]:
        onehot = onehot + sel.astype(F32)
    colsum = jnp.sum(onehot, axis=0, keepdims=True)

    @pl.when((phase == 0) & (i == 0))
    def _():
        tot_ref[...] = jnp.zeros_like(tot_ref)

    @pl.when(phase == 0)
    def _():
        tot_ref[...] += colsum

    @pl.when((phase == 1) & (i == 0))
    def _():
        tot = tot_ref[...]
        padded = jnp.floor((tot + (SUB_ROWS - 1)) / SUB_ROWS) * SUB_ROWS
        r = lax.broadcasted_iota(I32, (LANES, LANES), 0)
        c = lax.broadcasted_iota(I32, (LANES, LANES), 1)
        col = jnp.sum(jnp.where(r == c, jnp.broadcast_to(padded, (LANES, LANES)), 0.0), axis=1, keepdims=True)
        start_ref[...] = jnp.sum(jnp.where(r < c, col, 0.0), axis=0, keepdims=True)
        run_ref[...] = jnp.zeros_like(run_ref)
        cnt_ref[...] = tot

    @pl.when(phase == 1)
    def _():
        r = lax.broadcasted_iota(I32, (tt, tt), 0)
        c = lax.broadcasted_iota(I32, (tt, tt), 1)
        earlier = (r > c).astype(BF16)
        prefix = jnp.dot(earlier, onehot.astype(BF16), preferred_element_type=F32)
        base = prefix + run_ref[...] + start_ref[...]
        dest = [jnp.sum(jnp.where(sel, base, 0.0), axis=1, keepdims=True) for sel in sels]
        dest_ref[...] = jnp.concatenate(dest, axis=1).astype(I32)
        ex = [jnp.exp(vk - vals[0]) for vk in vals]
        den = ex[0] + ex[1] + ex[2] + ex[3]
        gate_ref[...] = jnp.concatenate([e / den for e in ex], axis=1)
        run_ref[...] += colsum


def _route(logits):
    t = logits.shape[0]
    tt = ROUTE_ROWS
    return pl.pallas_call(
        _route_body,
        grid=(2, t // tt),
        in_specs=[pl.BlockSpec((tt, LANES), lambda p, i: (i, 0))],
        out_specs=[pl.BlockSpec((tt, TOP_K), lambda p, i: (i * p, 0)),
                   pl.BlockSpec((tt, TOP_K), lambda p, i: (i * p, 0)),
                   pl.BlockSpec((1, LANES), lambda p, i: (0, 0))],
        out_shape=[jax.ShapeDtypeStruct((t, TOP_K), I32),
                   jax.ShapeDtypeStruct((t, TOP_K), F32),
                   jax.ShapeDtypeStruct((1, LANES), F32)],
        scratch_shapes=[pltpu.VMEM((1, LANES), F32)] * 3,
        compiler_params=_params(2),
        name="route",
    )(logits)


def _dispatch_body(dest_sm, h2p_ref, rows_in_ref, rows_ref, sem):
    del rows_in_ref
    td = h2p_ref.shape[0]
    base = pl.program_id(0) * (td * TOP_K)

    def row_copy(t, d):
        return pltpu.make_async_copy(h2p_ref.at[pl.ds(t, 1), :], rows_ref.at[pl.ds(d, 1), :], sem)

    def issue(t, carry):
        for k in range(TOP_K):
            row_copy(t, dest_sm[base + t * TOP_K + k]).start()
        return carry

    lax.fori_loop(0, td, issue, 0)

    def drain(t, carry):
        for k in range(TOP_K):
            row_copy(0, 0).wait()
        return carry

    lax.fori_loop(0, td, drain, 0)


def _dispatch(dest_flat, h2p, n_rows):
    t = h2p.shape[0]
    td = DISPATCH_ROWS
    rows0 = jnp.zeros((n_rows, HALF), U32)
    return pl.pallas_call(
        _dispatch_body,
        grid_spec=pltpu.PrefetchScalarGridSpec(
            num_scalar_prefetch=1,
            grid=(t // td,),
            in_specs=[pl.BlockSpec((td, HALF), lambda i, d: (i, 0)),
                      pl.BlockSpec(memory_space=pl.ANY)],
            out_specs=pl.BlockSpec(memory_space=pl.ANY),
            scratch_shapes=[pltpu.SemaphoreType.DMA],
        ),
        out_shape=jax.ShapeDtypeStruct((n_rows, HALF), U32),
        input_output_aliases={2: 0},
        compiler_params=_params(1),
        name="dispatch",
    )(dest_flat, h2p, rows0)


def _expert_body(vt, ve, vlo, vhi, vfirst, x_ref, w1g_ref, w1l_ref, b1g_ref, b1l_ref, w2_ref, b2_ref,
                 o_ref, xb_ref, w1g_s, w1l_s, w2_s):
    del vt, ve
    v = pl.program_id(0)
    j = pl.program_id(1)
    lo = vlo[v]
    hi = vhi[v]

    @pl.when((j == 0) & (vfirst[v] == 1))
    def _():
        o_ref[...] = jnp.zeros_like(o_ref)

    @pl.when(hi > lo)
    def _():
        w1g_s[...] = w1g_ref[...].astype(BF16)
        w1l_s[...] = w1l_ref[...].astype(BF16)
        w2_s[...] = w2_ref[...].astype(BF16)

        @pl.when(j == 0)
        def _():
            def unpack(sb, carry):
                rows = pl.ds(pl.multiple_of(sb * SUB_ROWS, SUB_ROWS), SUB_ROWS)
                x_lo, x_hi = _unpack_bf16_pair(x_ref[rows, :])
                xb_ref[rows, 0:HALF] = x_lo
                xb_ref[rows, HALF:] = x_hi
                o_ref[rows, :] = jnp.broadcast_to(b2_ref[...], (SUB_ROWS, D_MODEL))
                return carry

            lax.fori_loop(lo, hi, unpack, 0)

        def block(sb, carry):
            rows = pl.ds(pl.multiple_of(sb * SUB_ROWS, SUB_ROWS), SUB_ROWS)
            xb = xb_ref[rows, :]
            a_g = jnp.dot(xb, w1g_s[...], preferred_element_type=F32) + b1g_ref[...]
            a_l = jnp.dot(xb, w1l_s[...], preferred_element_type=F32) + b1l_ref[...]
            glu = jnp.minimum(a_g, SWIGLU_LIMIT)
            lin = jnp.clip(a_l, -SWIGLU_LIMIT, SWIGLU_LIMIT)
            act = glu * jax.nn.sigmoid(SWIGLU_ALPHA * glu) * (lin + 1.0)
            o_ref[rows, :] += jnp.dot(act.astype(BF16), w2_s[...], preferred_element_type=F32)
            return carry

        lax.fori_loop(lo, hi, block, 0)


def _experts(tables, rows, w1, b1, w2, b2):
    n_rows = rows.shape[0]
    n_visits = tables[0].shape[0]
    nj = N_FCHUNKS

    def jeff(j, vlo, vhi, v):
        return jnp.where(vhi[v] > vlo[v], j, nj - 1)

    return pl.pallas_call(
        _expert_body,
        grid_spec=pltpu.PrefetchScalarGridSpec(
            num_scalar_prefetch=5,
            grid=(n_visits, nj),
            in_specs=[
                pl.BlockSpec((TILE_ROWS, HALF), lambda v, j, vt, ve, vlo, vhi, vf: (vt[v], 0)),
                pl.BlockSpec((None, D_MODEL, F_CHUNK),
                             lambda v, j, vt, ve, vlo, vhi, vf: (ve[v], 0, jeff(j, vlo, vhi, v))),
                pl.BlockSpec((None, D_MODEL, F_CHUNK),
                             lambda v, j, vt, ve, vlo, vhi, vf: (ve[v], 0, nj + jeff(j, vlo, vhi, v))),
                pl.BlockSpec((None, 1, F_CHUNK),
                             lambda v, j, vt, ve, vlo, vhi, vf: (ve[v], 0, jeff(j, vlo, vhi, v))),
                pl.BlockSpec((None, 1, F_CHUNK),
                             lambda v, j, vt, ve, vlo, vhi, vf: (ve[v], 0, nj + jeff(j, vlo, vhi, v))),
                pl.BlockSpec((None, F_CHUNK, D_MODEL),
                             lambda v, j, vt, ve, vlo, vhi, vf: (ve[v], jeff(j, vlo, vhi, v), 0)),
                pl.BlockSpec((None, 1, D_MODEL), lambda v, j, vt, ve, vlo, vhi, vf: (ve[v], 0, 0)),
            ],
            out_specs=pl.BlockSpec((TILE_ROWS, D_MODEL), lambda v, j, vt, ve, vlo, vhi, vf: (vt[v], 0)),
            scratch_shapes=[pltpu.VMEM((TILE_ROWS, D_MODEL), BF16),
                            pltpu.VMEM((D_MODEL, F_CHUNK), BF16),
                            pltpu.VMEM((D_MODEL, F_CHUNK), BF16),
                            pltpu.VMEM((F_CHUNK, D_MODEL), BF16)],
        ),
        out_shape=jax.ShapeDtypeStruct((n_rows, D_MODEL), F32),
        compiler_params=_params(2),
        name="experts",
    )(*tables, rows, w1, w1, b1, b1, w2, b2)


def _visit_tables(counts, n_tiles):
    n_visits = n_tiles + N_EXPERTS
    nblk = (counts + (SUB_ROWS - 1)) // SUB_ROWS
    blk_end = jnp.cumsum(nblk)
    blk_start = blk_end - nblk
    first_tile = blk_start // TILE_SUBS
    last_tile = (blk_end - 1) // TILE_SUBS
    nvis = jnp.where(nblk > 0, last_tile - first_tile + 1, 0)
    vis_end = jnp.cumsum(nvis)
    vis_start = vis_end - nvis
    total = vis_end[-1]
    v = jnp.arange(n_visits, dtype=I32)
    valid = v < total
    v_eff = jnp.minimum(v, total - 1)
    e = jnp.minimum(jnp.searchsorted(vis_end, v_eff, side="right"), N_EXPERTS - 1).astype(I32)
    tile = first_tile[e] + (v_eff - vis_start[e])
    lo = jnp.maximum(blk_start[e], tile * TILE_SUBS) - tile * TILE_SUBS
    hi = jnp.minimum(blk_end[e], (tile + 1) * TILE_SUBS) - tile * TILE_SUBS
    lo = jnp.where(valid, lo, 0)
    hi = jnp.where(valid, hi, 0)
    prev_tile = jnp.concatenate([jnp.full((1,), -1, I32), tile[:-1]])
    first = (valid & (tile != prev_tile)).astype(I32)
    return tuple(a.astype(I32) for a in (tile, e, lo, hi, first))


def _combine_body(dest_sm, rows_ref, x1_ref, gate_ref, mod_ref, fg_ref, o_ref, buf, sem, *, n_tiles):
    tc = x1_ref.shape[0]
    i = pl.program_id(0)

    def row_copy(d, slot, k, t):
        return pltpu.make_async_copy(rows_ref.at[pl.ds(d, 1), :], buf.at[slot, k, pl.ds(t, 1), :], sem.at[slot])

    def issue(tile, slot):
        base = tile * (tc * TOP_K)

        def one(t, carry):
            for k in range(TOP_K):
                row_copy(dest_sm[base + t * TOP_K + k], slot, k, t).start()
            return carry

        lax.fori_loop(0, tc, one, 0)

    @pl.when(i == 0)
    def _():
        issue(0, 0)

    @pl.when(i + 1 < n_tiles)
    def _():
        issue(i + 1, (i + 1) % 2)

    slot = i % 2

    def drain(t, carry):
        for k in range(TOP_K):
            row_copy(0, slot, k, 0).wait()
        return carry

    lax.fori_loop(0, tc, drain, 0)

    g = gate_ref[...]
    y = g[:, 0:1] * buf[slot, 0]
    for k in range(1, TOP_K):
        y = y + g[:, k:k + 1] * buf[slot, k]
    gate_f = mod_ref[:, 5 * D_MODEL:6 * D_MODEL]
    o_ref[...] = _rms(x1_ref[...] + gate_f * y, fg_ref[...])


def _combine(dest_flat, out_rows, x1, gates, mod3, seq, final_g):
    t, d = x1.shape
    tc = COMBINE_ROWS
    tps = seq // tc
    n_tiles = t // tc
    return pl.pallas_call(
        functools.partial(_combine_body, n_tiles=n_tiles),
        grid_spec=pltpu.PrefetchScalarGridSpec(
            num_scalar_prefetch=1,
            grid=(n_tiles,),
            in_specs=[pl.BlockSpec(memory_space=pl.ANY),
                      pl.BlockSpec((tc, d), lambda i, s: (i, 0)),
                      pl.BlockSpec((tc, TOP_K), lambda i, s: (i, 0)),
                      pl.BlockSpec((None, 1, mod3.shape[2]), lambda i, s: (i // tps, 0, 0)),
                      pl.BlockSpec((1, d), lambda i, s: (0, 0))],
            out_specs=pl.BlockSpec((tc, d), lambda i, s: (i, 0)),
            scratch_shapes=[pltpu.VMEM((2, TOP_K, tc, d), F32),
                            pltpu.SemaphoreType.DMA((2,))],
        ),
        out_shape=jax.ShapeDtypeStruct((t, d), F32),
        compiler_params=_params(1),
        name="combine",
    )(dest_flat, out_rows, x1, gates, mod3, final_g)


def kernel(x, c, mix_norm_g, w_ada, b_ada, w_in, gmlp_ln_g, gmlp_ln_b, gmlp_ws, gmlp_bs, pool_w, pool_scale,
           gmlp_out_g, pool_out_g, w_out, ffn_norm_g, router_w, router_b, moe_w1, moe_b1, moe_w2, moe_b2,
           final_norm_g):
    bsz, seq, d = x.shape
    t = bsz * seq
    assert d == D_MODEL and w_ada.shape[0] == 1, "single-layer block with d_model 2048"
    assert seq % MIX_ROWS == 0 and seq % COMBINE_ROWS == 0 and t % ROUTE_ROWS == 0 and t % DISPATCH_ROWS == 0
    row = lambda a: a.reshape(1, -1)

    c_pad = jnp.zeros((8, d), F32).at[:bsz].set(c)
    mod3 = _ada(c_pad, w_ada[0], row(b_ada[0]))[:bsz].reshape(bsz, 1, 6 * d)

    ws2 = jnp.tile(gmlp_ws[0], (1, 2, 2))
    bs2 = jnp.tile(gmlp_bs[0], (1, 2))[:, :, None]
    rw_hi = router_w[0].astype(BF16)
    rw_lo = (router_w[0] - rw_hi.astype(F32)).astype(BF16)
    lane_pad = lambda a: jnp.pad(a, ((0, 0), (0, LANES - a.shape[1])))
    rw = jnp.concatenate([lane_pad(rw_hi), lane_pad(rw_lo)], axis=1)
    rb = lane_pad(row(router_b[0]))

    x1, h2p, logits = _mix(
        x.reshape(t, d), mod3, seq, row(mix_norm_g[0]), w_in[0].astype(BF16), row(gmlp_ln_g[0]), row(gmlp_ln_b[0]),
        ws2, bs2, pool_w[0].astype(BF16), row(pool_scale[0]), row(gmlp_out_g[0]), row(pool_out_g[0]),
        w_out[0].astype(BF16), row(ffn_norm_g[0]), rw, rb)

    dest, gates, counts = _route(logits)
    dest_flat = dest.reshape(t * TOP_K)

    n_asg = t * TOP_K
    n_rows = -(-(n_asg + N_EXPERTS * SUB_ROWS) // TILE_ROWS) * TILE_ROWS
    rows = _dispatch(dest_flat, h2p, n_rows)

    tables = _visit_tables(counts[0, :N_EXPERTS].astype(I32), n_rows // TILE_ROWS)
    e, f = N_EXPERTS, EXPERT_DIM
    out_rows = _experts(tables, rows, moe_w1[0], moe_b1[0].reshape(e, 1, 2 * f), moe_w2[0],
                        moe_b2[0].reshape(e, 1, d))

    y = _combine(dest_flat, out_rows, x1, gates, mod3, seq, row(final_norm_g))
    return y.reshape(bsz, seq, d)
```

```python
import functools

import jax
import jax.numpy as jnp
from jax import lax
from jax.experimental import pallas as pl
from jax.experimental.pallas import tpu as pltpu

F32 = jnp.float32
BF16 = jnp.bfloat16
I32 = jnp.int32
U32 = jnp.uint32

D_MODEL = 2048
GMLP_WIDTH = 1024
HEAD_DIM = 128
N_HEADS = GMLP_WIDTH // HEAD_DIM
GMLP_BLOCK = 128
CHUNK = 64
POOL_WIDTH = 1024
POOL_WINDOWS = (2, 4, 8, 16)
POOL_GROUP_DIM = POOL_WIDTH // len(POOL_WINDOWS)
POOL_HALO = 16
N_EXPERTS = 32
TOP_K = 4
EXPERT_DIM = 2048
SWIGLU_ALPHA = 1.702
SWIGLU_LIMIT = 7.0
EPS = 1e-5

LANES = 128
SUB_ROWS = 256
TILE_SUBS = 4
TILE_ROWS = SUB_ROWS * TILE_SUBS
F_CHUNK = 256
N_FCHUNKS = EXPERT_DIM // F_CHUNK
HALF = D_MODEL // 2

MIX_ROWS = 256
ROUTE_ROWS = 512
DISPATCH_ROWS = 512
COMBINE_ROWS = 256
ADA_COLS = 1024

VMEM_LIMIT = 56 * 1024 * 1024


def _params(n_axes, vmem=VMEM_LIMIT):
    return pltpu.CompilerParams(dimension_semantics=("arbitrary",) * n_axes, vmem_limit_bytes=vmem)


def _rms(x, g):
    return x * lax.rsqrt(jnp.mean(x * x, axis=-1, keepdims=True) + EPS) * g


def _gelu(x):
    return 0.5 * x * (1.0 + lax.erf(x * (2.0 ** -0.5)))


def _ada_body(c_ref, w_ref, b_ref, o_ref):
    c = c_ref[...]
    cond = c * jax.nn.sigmoid(c)
    o_ref[...] = jnp.dot(cond.astype(BF16), w_ref[...].astype(BF16), preferred_element_type=F32) + b_ref[...]


def _ada(c_pad, w_ada, b_ada):
    rows, d = c_pad.shape
    n = w_ada.shape[1]
    return pl.pallas_call(
        _ada_body,
        grid=(n // ADA_COLS,),
        in_specs=[pl.BlockSpec((rows, d), lambda j: (0, 0)),
                  pl.BlockSpec((d, ADA_COLS), lambda j: (0, j)),
                  pl.BlockSpec((1, ADA_COLS), lambda j: (0, j))],
        out_specs=pl.BlockSpec((rows, ADA_COLS), lambda j: (0, j)),
        out_shape=jax.ShapeDtypeStruct((rows, n), F32),
        compiler_params=_params(1),
        name="ada",
    )(c_pad, w_ada, b_ada)


def _pack_bf16_pair(lo, hi):
    lo_bits = pltpu.bitcast(lo.astype(BF16).astype(F32), U32)
    hi_bits = pltpu.bitcast(hi.astype(BF16).astype(F32), U32)
    return hi_bits | (lo_bits >> 16)


def _unpack_bf16_pair(p):
    lo = pltpu.bitcast(p << 16, F32).astype(BF16)
    hi = pltpu.bitcast(p & jnp.uint32(0xFFFF0000), F32).astype(BF16)
    return lo, hi


def _mix_body(x_ref, mod_ref, mixg_ref, win_ref, lng_ref, lnb_ref, ws_ref, bs_ref, pw_ref, pscale_ref,
              gog_ref, pog_ref, wout_ref, ffng_ref, rw_ref, rb_ref,
              x1_ref, h2p_ref, logit_ref,
              pe_ref, ab_ref, cat_ref, *, tiles_per_seq):
    tr = x_ref.shape[0]
    d = D_MODEL
    seq_tile = pl.program_id(0) % tiles_per_seq
    x = x_ref[...]
    shift_m = mod_ref[:, 0 * d:1 * d]
    scale_m = mod_ref[:, 1 * d:2 * d]
    gate_m = mod_ref[:, 2 * d:3 * d]
    shift_f = mod_ref[:, 3 * d:4 * d]
    scale_f = mod_ref[:, 4 * d:5 * d]

    hb = (_rms(x, mixg_ref[...]) * (1.0 + scale_m) + shift_m).astype(BF16)

    u = _gelu(jnp.dot(hb, win_ref[:, 0:GMLP_WIDTH], preferred_element_type=F32))
    v = _gelu(jnp.dot(hb, win_ref[:, GMLP_WIDTH:2 * GMLP_WIDTH], preferred_element_type=F32))
    slab = 2 * GMLP_BLOCK
    ri = lax.broadcasted_iota(I32, (slab, slab), 0)
    ci = lax.broadcasted_iota(I32, (slab, slab), 1)
    same_block = (ri // GMLP_BLOCK) == (ci // GMLP_BLOCK)
    causal = ((ri % GMLP_BLOCK) // CHUNK) >= ((ci % GMLP_BLOCK) // CHUNK)
    keep = same_block & causal
    for h in range(N_HEADS):
        sl = slice(h * HEAD_DIM, (h + 1) * HEAD_DIM)
        vh = v[:, sl]
        dv = vh - jnp.mean(vh, axis=-1, keepdims=True)
        var = jnp.mean(dv * dv, axis=-1, keepdims=True)
        vn = (dv * lax.rsqrt(var + EPS) * lng_ref[:, sl] + lnb_ref[:, sl]).astype(BF16)
        w_sp = jnp.where(keep, ws_ref[h], 0.0).astype(BF16)
        for s in range(tr // slab):
            rows = slice(s * slab, (s + 1) * slab)
            mixed = jnp.dot(w_sp, vn[rows], preferred_element_type=F32) + bs_ref[h]
            a = u[rows, sl] * mixed
            ab_ref[rows, sl] = a
    a_all = ab_ref[...]
    ssq_a = jnp.sum(a_all * a_all, axis=-1, keepdims=True)
    cat_ref[:, 0:GMLP_WIDTH] = (a_all * lax.rsqrt(ssq_a / GMLP_WIDTH + EPS) * gog_ref[...]).astype(BF16)

    p = jnp.dot(hb, win_ref[:, 2 * GMLP_WIDTH:], preferred_element_type=F32)

    @pl.when(seq_tile == 0)
    def _():
        pe_ref[0:POOL_HALO, :] = jnp.zeros((POOL_HALO, POOL_WIDTH), F32)

    pe_ref[POOL_HALO:, :] = p
    pos1 = (seq_tile * tr + lax.broadcasted_iota(I32, (tr, 1), 0) + 1).astype(F32)
    for g, w in enumerate(POOL_WINDOWS):
        cs = slice(g * POOL_GROUP_DIM, (g + 1) * POOL_GROUP_DIM)
        e = pe_ref[:, cs]
        s = e
        shift = 1
        while shift < w:
            s = s + pltpu.roll(s, shift, 0)
            shift *= 2
        inv = 1.0 / jnp.minimum(pos1, float(w))
        pooled = s[POOL_HALO:] * inv - e[POOL_HALO:]
        y = jnp.dot(pooled.astype(BF16), pw_ref[g], preferred_element_type=F32) * pscale_ref[:, cs]
        ab_ref[:, cs] = y
    pe_ref[0:POOL_HALO, :] = pe_ref[tr:tr + POOL_HALO, :]
    b_all = ab_ref[...]
    ssq_b = jnp.sum(b_all * b_all, axis=-1, keepdims=True)
    cat_ref[:, GMLP_WIDTH:] = (b_all * lax.rsqrt(ssq_b / POOL_WIDTH + EPS) * pog_ref[...]).astype(BF16)

    x1 = x + gate_m * jnp.dot(cat_ref[...], wout_ref[...], preferred_element_type=F32)
    x1_ref[...] = x1
    h2 = _rms(x1, ffng_ref[...]) * (1.0 + scale_f) + shift_f
    h2p_ref[...] = _pack_bf16_pair(h2[:, :HALF], h2[:, HALF:])
    h_hi = h2.astype(BF16)
    h_lo = (h2 - h_hi.astype(F32)).astype(BF16)
    l_hi = jnp.dot(h_hi, rw_ref[...], preferred_element_type=F32)
    l_lo = jnp.dot(h_lo, rw_ref[:, 0:LANES], preferred_element_type=F32)
    logit_ref[...] = l_hi[:, 0:LANES] + l_hi[:, LANES:] + l_lo + rb_ref[...]


def _const_spec(shape):
    nd = len(shape)
    return pl.BlockSpec(shape, lambda i: (0,) * nd, pipeline_mode=pl.Buffered(1))


def _mix(x2d, mod3, seq, mix_g, w_in, ln_g, ln_b, ws2, bs2, pool_w, pool_scale, go_g, po_g, w_out, ffn_g, rw, rb):
    t, d = x2d.shape
    tr = MIX_ROWS
    tps = seq // tr
    row_spec = lambda cols: pl.BlockSpec((tr, cols), lambda i: (i, 0))
    consts = [mix_g, w_in, ln_g, ln_b, ws2, bs2, pool_w, pool_scale, go_g, po_g, w_out, ffn_g, rw, rb]
    return pl.pallas_call(
        functools.partial(_mix_body, tiles_per_seq=tps),
        grid=(t // tr,),
        in_specs=[row_spec(d),
                  pl.BlockSpec((None, 1, mod3.shape[2]), lambda i: (i // tps, 0, 0))]
                 + [_const_spec(a.shape) for a in consts],
        out_specs=[row_spec(d), row_spec(HALF), row_spec(LANES)],
        out_shape=[jax.ShapeDtypeStruct((t, d), F32),
                   jax.ShapeDtypeStruct((t, HALF), U32),
                   jax.ShapeDtypeStruct((t, LANES), F32)],
        scratch_shapes=[pltpu.VMEM((tr + POOL_HALO, POOL_WIDTH), F32),
                        pltpu.VMEM((tr, GMLP_WIDTH), F32),
                        pltpu.VMEM((tr, d), BF16)],
        compiler_params=_params(1),
        name="mix",
    )(x2d, mod3, *consts)


def _route_body(lg_ref, dest_ref, gate_ref, cnt_ref, tot_ref, run_ref, start_ref):
    phase = pl.program_id(0)
    i = pl.program_id(1)
    tt = lg_ref.shape[0]
    lane = lax.broadcasted_iota(I32, (tt, LANES), 1)
    l = jnp.where(lane < N_EXPERTS, lg_ref[...], -jnp.inf)
    sels, vals = [], []
    for _ in range(TOP_K):
        m = jnp.max(l, axis=1, keepdims=True)
        idx = jnp.min(jnp.where(l == m, lane, LANES), axis=1, keepdims=True)
        sel = lane == idx
        sels.append(sel)
        vals.append(m)
        l = jnp.where(sel, -jnp.inf, l)
    onehot = sels[0].astype(F32)
    for sel in sels[1:]:
        onehot = onehot + sel.astype(F32)
    colsum = jnp.sum(onehot, axis=0, keepdims=True)

    @pl.when((phase == 0) & (i == 0))
    def _():
        tot_ref[...] = jnp.zeros_like(tot_ref)

    @pl.when(phase == 0)
    def _():
        tot_ref[...] += colsum

    @pl.when((phase == 1) & (i == 0))
    def _():
        tot = tot_ref[...]
        padded = jnp.floor((tot + (SUB_ROWS - 1)) / SUB_ROWS) * SUB_ROWS
        r = lax.broadcasted_iota(I32, (LANES, LANES), 0)
        c = lax.broadcasted_iota(I32, (LANES, LANES), 1)
        col = jnp.sum(jnp.where(r == c, jnp.broadcast_to(padded, (LANES, LANES)), 0.0), axis=1, keepdims=True)
        start_ref[...] = jnp.sum(jnp.where(r < c, col, 0.0), axis=0, keepdims=True)
        run_ref[...] = jnp.zeros_like(run_ref)
        cnt_ref[...] = tot

    @pl.when(phase == 1)
    def _():
        r = lax.broadcasted_iota(I32, (tt, tt), 0)
        c = lax.broadcasted_iota(I32, (tt, tt), 1)
        earlier = (r > c).astype(BF16)
        prefix = jnp.dot(earlier, onehot.astype(BF16), preferred_element_type=F32)
        base = prefix + run_ref[...] + start_ref[...]
        dest = [jnp.sum(jnp.where(sel, base, 0.0), axis=1, keepdims=True) for sel in sels]
        dest_ref[...] = jnp.concatenate(dest, axis=1).astype(I32)
        ex = [jnp.exp(vk - vals[0]) for vk in vals]
        den = ex[0] + ex[1] + ex[2] + ex[3]
        gate_ref[...] = jnp.concatenate([e / den for e in ex], axis=1)
        run_ref[...] += colsum


def _route(logits):
    t = logits.shape[0]
    tt = ROUTE_ROWS
    return pl.pallas_call(
        _route_body,
        grid=(2, t // tt),
        in_specs=[pl.BlockSpec((tt, LANES), lambda p, i: (i, 0))],
        out_specs=[pl.BlockSpec((tt, TOP_K), lambda p, i: (i * p, 0)),
                   pl.BlockSpec((tt, TOP_K), lambda p, i: (i * p, 0)),
                   pl.BlockSpec((1, LANES), lambda p, i: (0, 0))],
        out_shape=[jax.ShapeDtypeStruct((t, TOP_K), I32),
                   jax.ShapeDtypeStruct((t, TOP_K), F32),
                   jax.ShapeDtypeStruct((1, LANES), F32)],
        scratch_shapes=[pltpu.VMEM((1, LANES), F32)] * 3,
        compiler_params=_params(2),
        name="route",
    )(logits)


def _dispatch_body(dest_sm, padfirst_sm, npad_sm, h2p_ref, rows_ref, zero_ref, sem, zsem):
    td = h2p_ref.shape[0]
    step = pl.program_id(0)
    base = step * (td * TOP_K)

    def issue(t, carry):
        for k in range(TOP_K):
            d = dest_sm[base + t * TOP_K + k]
            pltpu.make_async_copy(h2p_ref.at[pl.ds(t, 1), :], rows_ref.at[pl.ds(d, 1), :], sem).start(priority=k % 2)
        return carry

    lax.fori_loop(0, td, issue, 0, unroll=4)

    @pl.when(step == 0)
    def _():
        zero_ref[...] = jnp.zeros_like(zero_ref)

        def zero_copy(r):
            return pltpu.make_async_copy(zero_ref, rows_ref.at[pl.ds(r, 1), :], zsem)

        def per_range(e, carry):
            first = padfirst_sm[e]
            n_pad = npad_sm[e]

            def start(r, c):
                zero_copy(first + r).start()
                return c

            def wait(r, c):
                zero_copy(first + r).wait()
                return c

            lax.fori_loop(0, n_pad, start, 0)
            lax.fori_loop(0, n_pad, wait, 0)
            return carry

        lax.fori_loop(0, N_EXPERTS + 1, per_range, 0)

    for k in range(TOP_K):
        pltpu.make_async_copy(h2p_ref, rows_ref.at[pl.ds(0, td), :], sem).wait()


def _dispatch(dest_flat, pad_first, n_pad, h2p, n_rows):
    t = h2p.shape[0]
    td = DISPATCH_ROWS
    return pl.pallas_call(
        _dispatch_body,
        grid_spec=pltpu.PrefetchScalarGridSpec(
            num_scalar_prefetch=3,
            grid=(t // td,),
            in_specs=[pl.BlockSpec((td, HALF), lambda i, *_: (i, 0))],
            out_specs=pl.BlockSpec(memory_space=pl.ANY),
            scratch_shapes=[pltpu.VMEM((1, HALF), U32), pltpu.SemaphoreType.DMA, pltpu.SemaphoreType.DMA],
        ),
        out_shape=jax.ShapeDtypeStruct((n_rows, HALF), U32),
        compiler_params=_params(1),
        name="dispatch",
    )(dest_flat, pad_first, n_pad, h2p)


def _expert_body(vt, ve, vlo, vhi, vfirst, x_ref, w1g_ref, w1l_ref, b1g_ref, b1l_ref, w2_ref, b2_ref,
                 o_ref, w1g_s, w1l_s, w2_s):
    del vt, ve
    v = pl.program_id(0)
    j = pl.program_id(1)
    lo = vlo[v]
    hi = vhi[v]

    @pl.when((j == 0) & (vfirst[v] == 1))
    def _():
        o_ref[...] = jnp.zeros_like(o_ref)

    @pl.when(hi > lo)
    def _():
        w1g_s[...] = w1g_ref[...].astype(BF16)
        w1l_s[...] = w1l_ref[...].astype(BF16)
        w2_s[...] = w2_ref[...].astype(BF16)
        b2_first = jnp.where(j == 0, b2_ref[...], 0.0)

        def run(first_sub, n_sub):
            rows = pl.ds(pl.multiple_of(first_sub * SUB_ROWS, SUB_ROWS), n_sub * SUB_ROWS)
            x_lo, x_hi = _unpack_bf16_pair(x_ref[rows, :])
            a_g = (jnp.dot(x_lo, w1g_s[0:HALF, :], preferred_element_type=F32)
                   + jnp.dot(x_hi, w1g_s[HALF:, :], preferred_element_type=F32) + b1g_ref[...])
            a_l = (jnp.dot(x_lo, w1l_s[0:HALF, :], preferred_element_type=F32)
                   + jnp.dot(x_hi, w1l_s[HALF:, :], preferred_element_type=F32) + b1l_ref[...])
            glu = jnp.minimum(a_g, SWIGLU_LIMIT)
            lin = jnp.clip(a_l, -SWIGLU_LIMIT, SWIGLU_LIMIT)
            act = glu * jax.nn.sigmoid(SWIGLU_ALPHA * glu) * (lin + 1.0)
            o_ref[rows, :] += jnp.dot(act.astype(BF16), w2_s[...], preferred_element_type=F32) + b2_first

        n_sub = hi - lo

        def pair(p, carry):
            run(lo + 2 * p, 2)
            return carry

        lax.fori_loop(0, lax.shift_right_logical(n_sub, 1), pair, 0)

        @pl.when((n_sub & 1) == 1)
        def _():
            run(hi - 1, 1)


def _experts(tables, rows, w1, b1, w2, b2):
    n_rows = rows.shape[0]
    n_visits = tables[0].shape[0]
    nj = N_FCHUNKS

    def jeff(j, vlo, vhi, v):
        return jnp.where(vhi[v] > vlo[v], j, nj - 1)

    return pl.pallas_call(
        _expert_body,
        grid_spec=pltpu.PrefetchScalarGridSpec(
            num_scalar_prefetch=5,
            grid=(n_visits, nj),
            in_specs=[
                pl.BlockSpec((TILE_ROWS, HALF), lambda v, j, vt, ve, vlo, vhi, vf: (vt[v], 0)),
                pl.BlockSpec((None, D_MODEL, F_CHUNK),
                             lambda v, j, vt, ve, vlo, vhi, vf: (ve[v], 0, jeff(j, vlo, vhi, v))),
                pl.BlockSpec((None, D_MODEL, F_CHUNK),
                             lambda v, j, vt, ve, vlo, vhi, vf: (ve[v], 0, nj + jeff(j, vlo, vhi, v))),
                pl.BlockSpec((None, 1, F_CHUNK),
                             lambda v, j, vt, ve, vlo, vhi, vf: (ve[v], 0, jeff(j, vlo, vhi, v))),
                pl.BlockSpec((None, 1, F_CHUNK),
                             lambda v, j, vt, ve, vlo, vhi, vf: (ve[v], 0, nj + jeff(j, vlo, vhi, v))),
                pl.BlockSpec((None, F_CHUNK, D_MODEL),
                             lambda v, j, vt, ve, vlo, vhi, vf: (ve[v], jeff(j, vlo, vhi, v), 0)),
                pl.BlockSpec((None, 1, D_MODEL), lambda v, j, vt, ve, vlo, vhi, vf: (ve[v], 0, 0)),
            ],
            out_specs=pl.BlockSpec((TILE_ROWS, D_MODEL), lambda v, j, vt, ve, vlo, vhi, vf: (vt[v], 0)),
            scratch_shapes=[pltpu.VMEM((D_MODEL, F_CHUNK), BF16),
                            pltpu.VMEM((D_MODEL, F_CHUNK), BF16),
                            pltpu.VMEM((F_CHUNK, D_MODEL), BF16)],
        ),
        out_shape=jax.ShapeDtypeStruct((n_rows, D_MODEL), F32),
        compiler_params=_params(2),
        name="experts",
    )(*tables, rows, w1, w1, b1, b1, w2, b2)


def _layout_tables(counts, n_tiles):
    nblk = (counts + (SUB_ROWS - 1)) // SUB_ROWS
    blk_end = jnp.cumsum(nblk)
    blk_start = blk_end - nblk
    used_rows = blk_end[-1:] * SUB_ROWS
    pad_first = jnp.concatenate([blk_start * SUB_ROWS + counts, used_rows])
    n_pad = jnp.concatenate([nblk * SUB_ROWS - counts, n_tiles * TILE_ROWS - used_rows])

    n_visits = n_tiles + N_EXPERTS
    first_tile = blk_start // TILE_SUBS
    last_tile = (blk_end - 1) // TILE_SUBS
    nvis = jnp.where(nblk > 0, last_tile - first_tile + 1, 0)
    vis_end = jnp.cumsum(nvis)
    vis_start = vis_end - nvis
    total = vis_end[-1]
    v = jnp.arange(n_visits, dtype=I32)
    valid = v < total
    v_eff = jnp.minimum(v, total - 1)
    e = jnp.minimum(jnp.sum((vis_end[None, :] <= v_eff[:, None]).astype(I32), axis=1), N_EXPERTS - 1)
    onehot = (e[:, None] == jnp.arange(N_EXPERTS, dtype=I32)[None, :]).astype(I32)
    pick = lambda a: jnp.sum(onehot * a[None, :], axis=1)
    tile = jnp.minimum(pick(first_tile) + (v_eff - pick(vis_start)) + (v - v_eff), n_tiles - 1)
    lo = jnp.maximum(pick(blk_start), tile * TILE_SUBS) - tile * TILE_SUBS
    hi = jnp.minimum(pick(blk_end), (tile + 1) * TILE_SUBS) - tile * TILE_SUBS
    lo = jnp.where(valid, lo, 0)
    hi = jnp.where(valid, hi, 0)
    prev_tile = jnp.concatenate([jnp.full((1,), -1, I32), tile[:-1]])
    first = (tile != prev_tile).astype(I32)
    visit = tuple(a.astype(I32) for a in (tile, e, lo, hi, first))
    return pad_first.astype(I32), n_pad.astype(I32), visit


def _combine_body(dest_sm, rows_ref, x1_ref, gate_ref, mod_ref, fg_ref, o_ref, buf, sem, *, n_tiles):
    tc = x1_ref.shape[0]
    i = pl.program_id(0)

    def row_copy(d, slot, k, t):
        return pltpu.make_async_copy(rows_ref.at[pl.ds(d, 1), :], buf.at[slot, k, pl.ds(t, 1), :], sem.at[slot])

    def issue(tile, slot):
        base = tile * (tc * TOP_K)

        def one(t, carry):
            for k in range(TOP_K):
                row_copy(dest_sm[base + t * TOP_K + k], slot, k, t).start(priority=k % 2)
            return carry

        lax.fori_loop(0, tc, one, 0, unroll=4)

    @pl.when(i == 0)
    def _():
        issue(0, 0)

    @pl.when(i + 1 < n_tiles)
    def _():
        issue(i + 1, (i + 1) % 2)

    slot = i % 2

    for k in range(TOP_K):
        pltpu.make_async_copy(rows_ref.at[pl.ds(0, tc), :], buf.at[slot, k], sem.at[slot]).wait()

    g = gate_ref[...]
    y = g[:, 0:1] * buf[slot, 0]
    for k in range(1, TOP_K):
        y = y + g[:, k:k + 1] * buf[slot, k]
    gate_f = mod_ref[:, 5 * D_MODEL:6 * D_MODEL]
    o_ref[...] = _rms(x1_ref[...] + gate_f * y, fg_ref[...])


def _combine(dest_flat, out_rows, x1, gates, mod3, seq, final_g):
    t, d = x1.shape
    tc = COMBINE_ROWS
    tps = seq // tc
    n_tiles = t // tc
    return pl.pallas_call(
        functools.partial(_combine_body, n_tiles=n_tiles),
        grid_spec=pltpu.PrefetchScalarGridSpec(
            num_scalar_prefetch=1,
            grid=(n_tiles,),
            in_specs=[pl.BlockSpec(memory_space=pl.ANY),
                      pl.BlockSpec((tc, d), lambda i, s: (i, 0)),
                      pl.BlockSpec((tc, TOP_K), lambda i, s: (i, 0)),
                      pl.BlockSpec((None, 1, mod3.shape[2]), lambda i, s: (i // tps, 0, 0)),
                      pl.BlockSpec((1, d), lambda i, s: (0, 0))],
            out_specs=pl.BlockSpec((tc, d), lambda i, s: (i, 0)),
            scratch_shapes=[pltpu.VMEM((2, TOP_K, tc, d), F32),
                            pltpu.SemaphoreType.DMA((2,))],
        ),
        out_shape=jax.ShapeDtypeStruct((t, d), F32),
        compiler_params=_params(1),
        name="combine",
    )(dest_flat, out_rows, x1, gates, mod3, final_g)


def kernel(x, c, mix_norm_g, w_ada, b_ada, w_in, gmlp_ln_g, gmlp_ln_b, gmlp_ws, gmlp_bs, pool_w, pool_scale,
           gmlp_out_g, pool_out_g, w_out, ffn_norm_g, router_w, router_b, moe_w1, moe_b1, moe_w2, moe_b2,
           final_norm_g):
    bsz, seq, d = x.shape
    t = bsz * seq
    assert d == D_MODEL and w_ada.shape[0] == 1, "single-layer block with d_model 2048"
    assert seq % MIX_ROWS == 0 and seq % COMBINE_ROWS == 0 and t % ROUTE_ROWS == 0 and t % DISPATCH_ROWS == 0
    row = lambda a: a.reshape(1, -1)

    c_pad = jnp.zeros((8, d), F32).at[:bsz].set(c)
    mod3 = _ada(c_pad, w_ada[0], row(b_ada[0]))[:bsz].reshape(bsz, 1, 6 * d)

    ws2 = jnp.tile(gmlp_ws[0], (1, 2, 2))
    bs2 = jnp.tile(gmlp_bs[0], (1, 2))[:, :, None]
    rw_hi = router_w[0].astype(BF16)
    rw_lo = (router_w[0] - rw_hi.astype(F32)).astype(BF16)
    lane_pad = lambda a: jnp.pad(a, ((0, 0), (0, LANES - a.shape[1])))
    rw = jnp.concatenate([lane_pad(rw_hi), lane_pad(rw_lo)], axis=1)
    rb = lane_pad(row(router_b[0]))

    x1, h2p, logits = _mix(
        x.reshape(t, d), mod3, seq, row(mix_norm_g[0]), w_in[0].astype(BF16), row(gmlp_ln_g[0]), row(gmlp_ln_b[0]),
        ws2, bs2, pool_w[0].astype(BF16), row(pool_scale[0]), row(gmlp_out_g[0]), row(pool_out_g[0]),
        w_out[0].astype(BF16), row(ffn_norm_g[0]), rw, rb)

    dest, gates, counts = _route(logits)
    dest_flat = dest.reshape(t * TOP_K)

    n_asg = t * TOP_K
    n_rows = -(-(n_asg + N_EXPERTS * SUB_ROWS) // TILE_ROWS) * TILE_ROWS
    pad_first, n_pad, tables = _layout_tables(counts[0, :N_EXPERTS].astype(I32), n_rows // TILE_ROWS)
    rows = _dispatch(dest_flat, pad_first, n_pad, h2p, n_rows)
    e, f = N_EXPERTS, EXPERT_DIM
    out_rows = _experts(tables, rows, moe_w1[0], moe_b1[0].reshape(e, 1, 2 * f), moe_w2[0],
                        moe_b2[0].reshape(e, 1, d))

    y = _combine(dest_flat, out_rows, x1, gates, mod3, seq, row(final_norm_g))
    return y.reshape(bsz, seq, d)
```

```python
import functools

import jax
import jax.numpy as jnp
from jax import lax
from jax.experimental import pallas as pl
from jax.experimental.pallas import tpu as pltpu

F32 = jnp.float32
BF16 = jnp.bfloat16
I32 = jnp.int32
U32 = jnp.uint32

D_MODEL = 2048
GMLP_WIDTH = 1024
HEAD_DIM = 128
N_HEADS = GMLP_WIDTH // HEAD_DIM
GMLP_BLOCK = 128
CHUNK = 64
POOL_WIDTH = 1024
POOL_WINDOWS = (2, 4, 8, 16)
POOL_GROUP_DIM = POOL_WIDTH // len(POOL_WINDOWS)
POOL_HALO = 16
N_EXPERTS = 32
TOP_K = 4
EXPERT_DIM = 2048
SWIGLU_ALPHA = 1.702
SWIGLU_LIMIT = 7.0
EPS = 1e-5

LANES = 128
SUB_ROWS = 256
TILE_SUBS = 4
TILE_ROWS = SUB_ROWS * TILE_SUBS
F_CHUNK = 512
N_FCHUNKS = EXPERT_DIM // F_CHUNK
HALF = D_MODEL // 2

MIX_ROWS = 256
ROUTE_ROWS = 512
DISPATCH_ROWS = 256
COMBINE_ROWS = 256
ADA_COLS = 1024

VMEM_LIMIT = 56 * 1024 * 1024


def _params(n_axes, vmem=VMEM_LIMIT):
    return pltpu.CompilerParams(dimension_semantics=("arbitrary",) * n_axes, vmem_limit_bytes=vmem)


def _rms(x, g):
    return x * lax.rsqrt(jnp.mean(x * x, axis=-1, keepdims=True) + EPS) * g


def _gelu(x):
    return 0.5 * x * (1.0 + lax.erf(x * (2.0 ** -0.5)))


def _ada_body(c_ref, w_ref, b_ref, o_ref):
    c = c_ref[...]
    cond = c * jax.nn.sigmoid(c)
    o_ref[...] = jnp.dot(cond.astype(BF16), w_ref[...].astype(BF16), preferred_element_type=F32) + b_ref[...]


def _ada(c_pad, w_ada, b_ada):
    rows, d = c_pad.shape
    n = w_ada.shape[1]
    return pl.pallas_call(
        _ada_body,
        grid=(n // ADA_COLS,),
        in_specs=[pl.BlockSpec((rows, d), lambda j: (0, 0)),
                  pl.BlockSpec((d, ADA_COLS), lambda j: (0, j)),
                  pl.BlockSpec((1, ADA_COLS), lambda j: (0, j))],
        out_specs=pl.BlockSpec((rows, ADA_COLS), lambda j: (0, j)),
        out_shape=jax.ShapeDtypeStruct((rows, n), F32),
        compiler_params=_params(1),
        name="ada",
    )(c_pad, w_ada, b_ada)


def _pack_bf16_pair(lo, hi):
    lo_bits = pltpu.bitcast(lo.astype(BF16).astype(F32), U32)
    hi_bits = pltpu.bitcast(hi.astype(BF16).astype(F32), U32)
    return hi_bits | (lo_bits >> 16)


def _unpack_bf16_pair(p):
    lo = pltpu.bitcast(p << 16, F32).astype(BF16)
    hi = pltpu.bitcast(p & jnp.uint32(0xFFFF0000), F32).astype(BF16)
    return lo, hi


def _mix_body(x_ref, mod_ref, mixg_ref, win_ref, lng_ref, lnb_ref, ws_ref, bs_ref, pw_ref, pscale_ref,
              gog_ref, pog_ref, wout_ref, ffng_ref, rw_ref, rb_ref,
              x1_ref, h2p_ref, logit_ref,
              pe_ref, ab_ref, cat_ref, *, tiles_per_seq):
    tr = x_ref.shape[0]
    d = D_MODEL
    seq_tile = pl.program_id(0) % tiles_per_seq
    x = x_ref[...]
    shift_m = mod_ref[:, 0 * d:1 * d]
    scale_m = mod_ref[:, 1 * d:2 * d]
    gate_m = mod_ref[:, 2 * d:3 * d]
    shift_f = mod_ref[:, 3 * d:4 * d]
    scale_f = mod_ref[:, 4 * d:5 * d]

    hb = (_rms(x, mixg_ref[...]) * (1.0 + scale_m) + shift_m).astype(BF16)

    u = _gelu(jnp.dot(hb, win_ref[:, 0:GMLP_WIDTH], preferred_element_type=F32))
    v = _gelu(jnp.dot(hb, win_ref[:, GMLP_WIDTH:2 * GMLP_WIDTH], preferred_element_type=F32))
    slab = 2 * GMLP_BLOCK
    ri = lax.broadcasted_iota(I32, (slab, slab), 0)
    ci = lax.broadcasted_iota(I32, (slab, slab), 1)
    same_block = (ri // GMLP_BLOCK) == (ci // GMLP_BLOCK)
    causal = ((ri % GMLP_BLOCK) // CHUNK) >= ((ci % GMLP_BLOCK) // CHUNK)
    keep = same_block & causal
    for h in range(N_HEADS):
        sl = slice(h * HEAD_DIM, (h + 1) * HEAD_DIM)
        vh = v[:, sl]
        dv = vh - jnp.mean(vh, axis=-1, keepdims=True)
        var = jnp.mean(dv * dv, axis=-1, keepdims=True)
        vn = (dv * lax.rsqrt(var + EPS) * lng_ref[:, sl] + lnb_ref[:, sl]).astype(BF16)
        w_sp = jnp.where(keep, ws_ref[h], 0.0).astype(BF16)
        for s in range(tr // slab):
            rows = slice(s * slab, (s + 1) * slab)
            mixed = jnp.dot(w_sp, vn[rows], preferred_element_type=F32) + bs_ref[h]
            a = u[rows, sl] * mixed
            ab_ref[rows, sl] = a
    a_all = ab_ref[...]
    ssq_a = jnp.sum(a_all * a_all, axis=-1, keepdims=True)
    cat_ref[:, 0:GMLP_WIDTH] = (a_all * lax.rsqrt(ssq_a / GMLP_WIDTH + EPS) * gog_ref[...]).astype(BF16)

    p = jnp.dot(hb, win_ref[:, 2 * GMLP_WIDTH:], preferred_element_type=F32)

    @pl.when(seq_tile == 0)
    def _():
        pe_ref[0:POOL_HALO, :] = jnp.zeros((POOL_HALO, POOL_WIDTH), F32)

    pe_ref[POOL_HALO:, :] = p
    pos1 = (seq_tile * tr + lax.broadcasted_iota(I32, (tr, 1), 0) + 1).astype(F32)
    for g, w in enumerate(POOL_WINDOWS):
        cs = slice(g * POOL_GROUP_DIM, (g + 1) * POOL_GROUP_DIM)
        e = pe_ref[:, cs]
        s = e
        shift = 1
        while shift < w:
            s = s + pltpu.roll(s, shift, 0)
            shift *= 2
        inv = 1.0 / jnp.minimum(pos1, float(w))
        pooled = s[POOL_HALO:] * inv - e[POOL_HALO:]
        y = jnp.dot(pooled.astype(BF16), pw_ref[g], preferred_element_type=F32) * pscale_ref[:, cs]
        ab_ref[:, cs] = y
    pe_ref[0:POOL_HALO, :] = pe_ref[tr:tr + POOL_HALO, :]
    b_all = ab_ref[...]
    ssq_b = jnp.sum(b_all * b_all, axis=-1, keepdims=True)
    cat_ref[:, GMLP_WIDTH:] = (b_all * lax.rsqrt(ssq_b / POOL_WIDTH + EPS) * pog_ref[...]).astype(BF16)

    x1 = x + gate_m * jnp.dot(cat_ref[...], wout_ref[...], preferred_element_type=F32)
    x1_ref[...] = x1
    h2 = _rms(x1, ffng_ref[...]) * (1.0 + scale_f) + shift_f
    h2p_ref[...] = _pack_bf16_pair(h2[:, :HALF], h2[:, HALF:])
    h_hi = h2.astype(BF16)
    h_lo = (h2 - h_hi.astype(F32)).astype(BF16)
    l_hi = jnp.dot(h_hi, rw_ref[...], preferred_element_type=F32)
    l_lo = jnp.dot(h_lo, rw_ref[:, 0:LANES], preferred_element_type=F32)
    logit_ref[...] = l_hi[:, 0:LANES] + l_hi[:, LANES:] + l_lo + rb_ref[...]


def _const_spec(shape):
    nd = len(shape)
    return pl.BlockSpec(shape, lambda i: (0,) * nd, pipeline_mode=pl.Buffered(1))


def _mix(x2d, mod3, seq, mix_g, w_in, ln_g, ln_b, ws2, bs2, pool_w, pool_scale, go_g, po_g, w_out, ffn_g, rw, rb):
    t, d = x2d.shape
    tr = MIX_ROWS
    tps = seq // tr
    row_spec = lambda cols: pl.BlockSpec((tr, cols), lambda i: (i, 0))
    consts = [mix_g, w_in, ln_g, ln_b, ws2, bs2, pool_w, pool_scale, go_g, po_g, w_out, ffn_g, rw, rb]
    return pl.pallas_call(
        functools.partial(_mix_body, tiles_per_seq=tps),
        grid=(t // tr,),
        in_specs=[row_spec(d),
                  pl.BlockSpec((None, 1, mod3.shape[2]), lambda i: (i // tps, 0, 0))]
                 + [_const_spec(a.shape) for a in consts],
        out_specs=[row_spec(d), row_spec(HALF), row_spec(LANES)],
        out_shape=[jax.ShapeDtypeStruct((t, d), F32),
                   jax.ShapeDtypeStruct((t, HALF), U32),
                   jax.ShapeDtypeStruct((t, LANES), F32)],
        scratch_shapes=[pltpu.VMEM((tr + POOL_HALO, POOL_WIDTH), F32),
                        pltpu.VMEM((tr, GMLP_WIDTH), F32),
                        pltpu.VMEM((tr, d), BF16)],
        compiler_params=_params(1),
        name="mix",
    )(x2d, mod3, *consts)


def _route_body(lg_ref, w2_ref, dest_ref, gate_ref, cnt_ref, w2b_ref, tot_ref, run_ref, start_ref):
    phase = pl.program_id(0)
    i = pl.program_id(1)
    tt = lg_ref.shape[0]
    w2b_ref[...] = w2_ref[...].astype(BF16)
    lane = lax.broadcasted_iota(I32, (tt, LANES), 1)
    l = jnp.where(lane < N_EXPERTS, lg_ref[...], -jnp.inf)
    sels, vals = [], []
    for _ in range(TOP_K):
        m = jnp.max(l, axis=1, keepdims=True)
        idx = jnp.min(jnp.where(l == m, lane, LANES), axis=1, keepdims=True)
        sel = lane == idx
        sels.append(sel)
        vals.append(m)
        l = jnp.where(sel, -jnp.inf, l)
    onehot = sels[0].astype(F32)
    for sel in sels[1:]:
        onehot = onehot + sel.astype(F32)
    colsum = jnp.sum(onehot, axis=0, keepdims=True)

    @pl.when((phase == 0) & (i == 0))
    def _():
        tot_ref[...] = jnp.zeros_like(tot_ref)

    @pl.when(phase == 0)
    def _():
        tot_ref[...] += colsum

    @pl.when((phase == 1) & (i == 0))
    def _():
        tot = tot_ref[...]
        padded = jnp.floor((tot + (SUB_ROWS - 1)) / SUB_ROWS) * SUB_ROWS
        r = lax.broadcasted_iota(I32, (LANES, LANES), 0)
        c = lax.broadcasted_iota(I32, (LANES, LANES), 1)
        col = jnp.sum(jnp.where(r == c, jnp.broadcast_to(padded, (LANES, LANES)), 0.0), axis=1, keepdims=True)
        start_ref[...] = jnp.sum(jnp.where(r < c, col, 0.0), axis=0, keepdims=True)
        run_ref[...] = jnp.zeros_like(run_ref)
        cnt_ref[...] = tot

    @pl.when(phase == 1)
    def _():
        r = lax.broadcasted_iota(I32, (tt, tt), 0)
        c = lax.broadcasted_iota(I32, (tt, tt), 1)
        earlier = (r > c).astype(BF16)
        prefix = jnp.dot(earlier, onehot.astype(BF16), preferred_element_type=F32)
        base = prefix + run_ref[...] + start_ref[...]
        dest = [jnp.sum(jnp.where(sel, base, 0.0), axis=1, keepdims=True) for sel in sels]
        dest_ref[...] = jnp.concatenate(dest, axis=1).astype(I32)
        ex = [jnp.exp(vk - vals[0]) for vk in vals]
        den = ex[0] + ex[1] + ex[2] + ex[3]
        gate_ref[...] = jnp.concatenate([e / den for e in ex], axis=1)
        run_ref[...] += colsum


def _route(logits, w2):
    t = logits.shape[0]
    tt = ROUTE_ROWS
    n_i = t // tt
    n_e, f, d = w2.shape
    slabs, rem = divmod(2 * n_i, n_e)
    assert rem == 0 and f % slabs == 0, "route steps must split the expert weights evenly"
    slab_spec = pl.BlockSpec((None, f // slabs, d), lambda p, i: ((p * n_i + i) // slabs, (p * n_i + i) % slabs, 0))
    return pl.pallas_call(
        _route_body,
        grid=(2, n_i),
        in_specs=[pl.BlockSpec((tt, LANES), lambda p, i: (i, 0)), slab_spec],
        out_specs=[pl.BlockSpec((tt, TOP_K), lambda p, i: (i * p, 0)),
                   pl.BlockSpec((tt, TOP_K), lambda p, i: (i * p, 0)),
                   pl.BlockSpec((1, LANES), lambda p, i: (0, 0)),
                   slab_spec],
        out_shape=[jax.ShapeDtypeStruct((t, TOP_K), I32),
                   jax.ShapeDtypeStruct((t, TOP_K), F32),
                   jax.ShapeDtypeStruct((1, LANES), F32),
                   jax.ShapeDtypeStruct(w2.shape, BF16)],
        scratch_shapes=[pltpu.VMEM((1, LANES), F32)] * 3,
        compiler_params=_params(2),
        name="route",
    )(logits, w2)


def _dispatch_body(dest_sm, padfirst_sm, npad_sm, h2p_ref, w1g_ref, w1l_ref, rows_ref, w1r_ref, zero_ref, sem, zsem):
    td = h2p_ref.shape[0]
    step = pl.program_id(0)
    base = step * (td * TOP_K)
    w1r_ref[:, 0:F_CHUNK] = w1g_ref[...].astype(BF16)
    w1r_ref[:, F_CHUNK:] = w1l_ref[...].astype(BF16)

    def issue(t, carry):
        for k in range(TOP_K):
            d = dest_sm[base + t * TOP_K + k]
            pltpu.make_async_copy(h2p_ref.at[pl.ds(t, 1), :], rows_ref.at[pl.ds(d, 1), :], sem).start(priority=k % 2)
        return carry

    lax.fori_loop(0, td, issue, 0, unroll=4)

    @pl.when(step == 0)
    def _():
        zero_ref[...] = jnp.zeros_like(zero_ref)

        def zero_copy(r):
            return pltpu.make_async_copy(zero_ref, rows_ref.at[pl.ds(r, 1), :], zsem)

        def per_range(e, carry):
            first = padfirst_sm[e]
            n_pad = npad_sm[e]

            def start(r, c):
                zero_copy(first + r).start()
                return c

            def wait(r, c):
                zero_copy(first + r).wait()
                return c

            lax.fori_loop(0, n_pad, start, 0)
            lax.fori_loop(0, n_pad, wait, 0)
            return carry

        lax.fori_loop(0, N_EXPERTS + 1, per_range, 0)

    for k in range(TOP_K):
        pltpu.make_async_copy(h2p_ref, rows_ref.at[pl.ds(0, td), :], sem).wait()


def _dispatch(dest_flat, pad_first, n_pad, h2p, w1, n_rows):
    t = h2p.shape[0]
    td = DISPATCH_ROWS
    n_e, d, f2 = w1.shape
    nj = N_FCHUNKS
    assert t // td == n_e * nj and f2 == 2 * nj * F_CHUNK, "one up-projection chunk per dispatch step"
    return pl.pallas_call(
        _dispatch_body,
        grid_spec=pltpu.PrefetchScalarGridSpec(
            num_scalar_prefetch=3,
            grid=(t // td,),
            in_specs=[pl.BlockSpec((td, HALF), lambda i, *_: (i, 0)),
                      pl.BlockSpec((None, d, F_CHUNK), lambda i, *_: (i // nj, 0, i % nj)),
                      pl.BlockSpec((None, d, F_CHUNK), lambda i, *_: (i // nj, 0, nj + i % nj))],
            out_specs=[pl.BlockSpec(memory_space=pl.ANY),
                       pl.BlockSpec((None, None, d, 2 * F_CHUNK), lambda i, *_: (i // nj, i % nj, 0, 0))],
            scratch_shapes=[pltpu.VMEM((1, HALF), U32), pltpu.SemaphoreType.DMA, pltpu.SemaphoreType.DMA],
        ),
        out_shape=[jax.ShapeDtypeStruct((n_rows, HALF), U32),
                   jax.ShapeDtypeStruct((n_e, nj, d, 2 * F_CHUNK), BF16)],
        compiler_params=_params(1),
        name="dispatch",
    )(dest_flat, pad_first, n_pad, h2p, w1, w1)


def _expert_body(vt, ve, vlo, vhi, vfirst, x_ref, w1_ref, b1g_ref, b1l_ref, w2_ref, b2_ref, o_ref):
    del vt, ve
    v = pl.program_id(0)
    j = pl.program_id(1)
    lo = vlo[v]
    hi = vhi[v]

    @pl.when((j == 0) & (vfirst[v] == 1))
    def _():
        o_ref[...] = jnp.zeros_like(o_ref)

    @pl.when(hi > lo)
    def _():
        b2_first = jnp.where(j == 0, b2_ref[...], 0.0)

        def run(first_sub, n_sub):
            rows = pl.ds(pl.multiple_of(first_sub * SUB_ROWS, SUB_ROWS), n_sub * SUB_ROWS)
            x_lo, x_hi = _unpack_bf16_pair(x_ref[rows, :])
            a = (jnp.dot(x_lo, w1_ref[0:HALF, :], preferred_element_type=F32)
                 + jnp.dot(x_hi, w1_ref[HALF:, :], preferred_element_type=F32))
            glu = jnp.minimum(a[:, 0:F_CHUNK] + b1g_ref[...], SWIGLU_LIMIT)
            lin = jnp.clip(a[:, F_CHUNK:] + b1l_ref[...], -SWIGLU_LIMIT, SWIGLU_LIMIT)
            act = glu * jax.nn.sigmoid(SWIGLU_ALPHA * glu) * (lin + 1.0)
            o_ref[rows, :] += jnp.dot(act.astype(BF16), w2_ref[...], preferred_element_type=F32) + b2_first

        n_sub = hi - lo

        def pair(p, carry):
            run(lo + 2 * p, 2)
            return carry

        lax.fori_loop(0, lax.shift_right_logical(n_sub, 1), pair, 0)

        @pl.when((n_sub & 1) == 1)
        def _():
            run(hi - 1, 1)


def _experts(tables, rows, w1r, b1, w2b, b2):
    n_rows = rows.shape[0]
    n_visits = tables[0].shape[0]
    nj = N_FCHUNKS

    def jeff(j, vlo, vhi, v):
        return jnp.where(vhi[v] > vlo[v], j, nj - 1)

    return pl.pallas_call(
        _expert_body,
        grid_spec=pltpu.PrefetchScalarGridSpec(
            num_scalar_prefetch=5,
            grid=(n_visits, nj),
            in_specs=[
                pl.BlockSpec((TILE_ROWS, HALF), lambda v, j, vt, ve, vlo, vhi, vf: (vt[v], 0)),
                pl.BlockSpec((None, None, D_MODEL, 2 * F_CHUNK),
                             lambda v, j, vt, ve, vlo, vhi, vf: (ve[v], jeff(j, vlo, vhi, v), 0, 0)),
                pl.BlockSpec((None, 1, F_CHUNK),
                             lambda v, j, vt, ve, vlo, vhi, vf: (ve[v], 0, jeff(j, vlo, vhi, v))),
                pl.BlockSpec((None, 1, F_CHUNK),
                             lambda v, j, vt, ve, vlo, vhi, vf: (ve[v], 0, nj + jeff(j, vlo, vhi, v))),
                pl.BlockSpec((None, F_CHUNK, D_MODEL),
                             lambda v, j, vt, ve, vlo, vhi, vf: (ve[v], jeff(j, vlo, vhi, v), 0)),
                pl.BlockSpec((None, 1, D_MODEL), lambda v, j, vt, ve, vlo, vhi, vf: (ve[v], 0, 0)),
            ],
            out_specs=pl.BlockSpec((TILE_ROWS, D_MODEL), lambda v, j, vt, ve, vlo, vhi, vf: (vt[v], 0)),
        ),
        out_shape=jax.ShapeDtypeStruct((n_rows, D_MODEL), F32),
        compiler_params=_params(2),
        name="experts",
    )(*tables, rows, w1r, b1, b1, w2b, b2)


def _layout_tables(counts, n_tiles):
    nblk = (counts + (SUB_ROWS - 1)) // SUB_ROWS
    blk_end = jnp.cumsum(nblk)
    blk_start = blk_end - nblk
    used_rows = blk_end[-1:] * SUB_ROWS
    pad_first = jnp.concatenate([blk_start * SUB_ROWS + counts, used_rows])
    n_pad = jnp.concatenate([nblk * SUB_ROWS - counts, n_tiles * TILE_ROWS - used_rows])

    n_visits = n_tiles + N_EXPERTS
    first_tile = blk_start // TILE_SUBS
    last_tile = (blk_end - 1) // TILE_SUBS
    nvis = jnp.where(nblk > 0, last_tile - first_tile + 1, 0)
    vis_end = jnp.cumsum(nvis)
    vis_start = vis_end - nvis
    total = vis_end[-1]
    v = jnp.arange(n_visits, dtype=I32)
    valid = v < total
    v_eff = jnp.minimum(v, total - 1)
    e = jnp.minimum(jnp.sum((vis_end[None, :] <= v_eff[:, None]).astype(I32), axis=1), N_EXPERTS - 1)
    onehot = (e[:, None] == jnp.arange(N_EXPERTS, dtype=I32)[None, :]).astype(I32)
    pick = lambda a: jnp.sum(onehot * a[None, :], axis=1)
    tile = jnp.minimum(pick(first_tile) + (v_eff - pick(vis_start)) + (v - v_eff), n_tiles - 1)
    lo = jnp.maximum(pick(blk_start), tile * TILE_SUBS) - tile * TILE_SUBS
    hi = jnp.minimum(pick(blk_end), (tile + 1) * TILE_SUBS) - tile * TILE_SUBS
    lo = jnp.where(valid, lo, 0)
    hi = jnp.where(valid, hi, 0)
    prev_tile = jnp.concatenate([jnp.full((1,), -1, I32), tile[:-1]])
    first = (tile != prev_tile).astype(I32)
    visit = tuple(a.astype(I32) for a in (tile, e, lo, hi, first))
    return pad_first.astype(I32), n_pad.astype(I32), visit


def _combine_body(dest_sm, rows_ref, x1_ref, gate_ref, mod_ref, fg_ref, o_ref, buf, sem, *, n_tiles):
    tc = x1_ref.shape[0]
    i = pl.program_id(0)

    def row_copy(d, slot, k, t):
        return pltpu.make_async_copy(rows_ref.at[pl.ds(d, 1), :], buf.at[slot, k, pl.ds(t, 1), :], sem.at[slot])

    def issue(tile, slot):
        base = tile * (tc * TOP_K)

        def one(t, carry):
            for k in range(TOP_K):
                row_copy(dest_sm[base + t * TOP_K + k], slot, k, t).start(priority=k % 2)
            return carry

        lax.fori_loop(0, tc, one, 0, unroll=4)

    @pl.when(i == 0)
    def _():
        issue(0, 0)

    @pl.when(i + 1 < n_tiles)
    def _():
        issue(i + 1, (i + 1) % 2)

    slot = i % 2

    for k in range(TOP_K):
        pltpu.make_async_copy(rows_ref.at[pl.ds(0, tc), :], buf.at[slot, k], sem.at[slot]).wait()

    g = gate_ref[...]
    y = g[:, 0:1] * buf[slot, 0]
    for k in range(1, TOP_K):
        y = y + g[:, k:k + 1] * buf[slot, k]
    gate_f = mod_ref[:, 5 * D_MODEL:6 * D_MODEL]
    o_ref[...] = _rms(x1_ref[...] + gate_f * y, fg_ref[...])


def _combine(dest_flat, out_rows, x1, gates, mod3, seq, final_g):
    t, d = x1.shape
    tc = COMBINE_ROWS
    tps = seq // tc
    n_tiles = t // tc
    return pl.pallas_call(
        functools.partial(_combine_body, n_tiles=n_tiles),
        grid_spec=pltpu.PrefetchScalarGridSpec(
            num_scalar_prefetch=1,
            grid=(n_tiles,),
            in_specs=[pl.BlockSpec(memory_space=pl.ANY),
                      pl.BlockSpec((tc, d), lambda i, s: (i, 0)),
                      pl.BlockSpec((tc, TOP_K), lambda i, s: (i, 0)),
                      pl.BlockSpec((None, 1, mod3.shape[2]), lambda i, s: (i // tps, 0, 0)),
                      pl.BlockSpec((1, d), lambda i, s: (0, 0))],
            out_specs=pl.BlockSpec((tc, d), lambda i, s: (i, 0)),
            scratch_shapes=[pltpu.VMEM((2, TOP_K, tc, d), F32),
                            pltpu.SemaphoreType.DMA((2,))],
        ),
        out_shape=jax.ShapeDtypeStruct((t, d), F32),
        compiler_params=_params(1),
        name="combine",
    )(dest_flat, out_rows, x1, gates, mod3, final_g)


def kernel(x, c, mix_norm_g, w_ada, b_ada, w_in, gmlp_ln_g, gmlp_ln_b, gmlp_ws, gmlp_bs, pool_w, pool_scale,
           gmlp_out_g, pool_out_g, w_out, ffn_norm_g, router_w, router_b, moe_w1, moe_b1, moe_w2, moe_b2,
           final_norm_g):
    bsz, seq, d = x.shape
    t = bsz * seq
    assert d == D_MODEL and w_ada.shape[0] == 1, "single-layer block with d_model 2048"
    assert seq % MIX_ROWS == 0 and seq % COMBINE_ROWS == 0 and t % ROUTE_ROWS == 0 and t % DISPATCH_ROWS == 0
    row = lambda a: a.reshape(1, -1)

    c_pad = jnp.zeros((8, d), F32).at[:bsz].set(c)
    mod3 = _ada(c_pad, w_ada[0], row(b_ada[0]))[:bsz].reshape(bsz, 1, 6 * d)

    ws2 = jnp.tile(gmlp_ws[0], (1, 2, 2))
    bs2 = jnp.tile(gmlp_bs[0], (1, 2))[:, :, None]
    rw_hi = router_w[0].astype(BF16)
    rw_lo = (router_w[0] - rw_hi.astype(F32)).astype(BF16)
    lane_pad = lambda a: jnp.pad(a, ((0, 0), (0, LANES - a.shape[1])))
    rw = jnp.concatenate([lane_pad(rw_hi), lane_pad(rw_lo)], axis=1)
    rb = lane_pad(row(router_b[0]))

    x1, h2p, logits = _mix(
        x.reshape(t, d), mod3, seq, row(mix_norm_g[0]), w_in[0].astype(BF16), row(gmlp_ln_g[0]), row(gmlp_ln_b[0]),
        ws2, bs2, pool_w[0].astype(BF16), row(pool_scale[0]), row(gmlp_out_g[0]), row(pool_out_g[0]),
        w_out[0].astype(BF16), row(ffn_norm_g[0]), rw, rb)

    dest, gates, counts, w2b = _route(logits, moe_w2[0])
    dest_flat = dest.reshape(t * TOP_K)

    n_asg = t * TOP_K
    n_rows = -(-(n_asg + N_EXPERTS * SUB_ROWS) // TILE_ROWS) * TILE_ROWS
    pad_first, n_pad, tables = _layout_tables(counts[0, :N_EXPERTS].astype(I32), n_rows // TILE_ROWS)
    rows, w1r = _dispatch(dest_flat, pad_first, n_pad, h2p, moe_w1[0], n_rows)
    e, f = N_EXPERTS, EXPERT_DIM
    out_rows = _experts(tables, rows, w1r, moe_b1[0].reshape(e, 1, 2 * f), w2b, moe_b2[0].reshape(e, 1, d))

    y = _combine(dest_flat, out_rows, x1, gates, mod3, seq, row(final_norm_g))
    return y.reshape(bsz, seq, d)
```

```python
import functools

import jax
import jax.numpy as jnp
from jax import lax
from jax.experimental import pallas as pl
from jax.experimental.pallas import tpu as pltpu

F32 = jnp.float32
BF16 = jnp.bfloat16
I32 = jnp.int32
U32 = jnp.uint32

D_MODEL = 2048
GMLP_WIDTH = 1024
HEAD_DIM = 128
N_HEADS = GMLP_WIDTH // HEAD_DIM
GMLP_BLOCK = 128
CHUNK = 64
POOL_WIDTH = 1024
POOL_WINDOWS = (2, 4, 8, 16)
POOL_GROUP_DIM = POOL_WIDTH // len(POOL_WINDOWS)
POOL_HALO = 16
N_EXPERTS = 32
TOP_K = 4
EXPERT_DIM = 2048
SWIGLU_ALPHA = 1.702
SWIGLU_LIMIT = 7.0
EPS = 1e-5

LANES = 128
SUBLANES = 8
SUB_ROWS = 256
TILE_SUBS = 4
TILE_ROWS = SUB_ROWS * TILE_SUBS
F_CHUNK = 512
N_FCHUNKS = EXPERT_DIM // F_CHUNK
HALF = D_MODEL // 2

MIX_ROWS = 256
ROUTE_ROWS = 512
DISPATCH_ROWS = 256
COMBINE_ROWS = 256
ADA_COLS = 1024

VMEM_LIMIT = 56 * 1024 * 1024


def _params(n_axes, vmem=VMEM_LIMIT):
    return pltpu.CompilerParams(dimension_semantics=("arbitrary",) * n_axes, vmem_limit_bytes=vmem)


def _rms(x, g):
    return x * lax.rsqrt(jnp.mean(x * x, axis=-1, keepdims=True) + EPS) * g


def _gelu(x):
    return 0.5 * x * (1.0 + lax.erf(x * (2.0 ** -0.5)))


def _ada_body(c_ref, w_ref, b_ref, o_ref):
    c = c_ref[...]
    cond = c * jax.nn.sigmoid(c)
    o_ref[...] = jnp.dot(cond.astype(BF16), w_ref[...].astype(BF16), preferred_element_type=F32) + b_ref[...]


def _ada(c_pad, w_ada, b_ada):
    rows, d = c_pad.shape
    n = w_ada.shape[1]
    return pl.pallas_call(
        _ada_body,
        grid=(n // ADA_COLS,),
        in_specs=[pl.BlockSpec((rows, d), lambda j: (0, 0)),
                  pl.BlockSpec((d, ADA_COLS), lambda j: (0, j)),
                  pl.BlockSpec((1, ADA_COLS), lambda j: (0, j))],
        out_specs=pl.BlockSpec((rows, ADA_COLS), lambda j: (0, j)),
        out_shape=jax.ShapeDtypeStruct((rows, n), F32),
        compiler_params=_params(1),
        name="ada",
    )(c_pad, w_ada, b_ada)


def _pack_bf16_pair(lo, hi):
    lo_bits = pltpu.bitcast(lo.astype(BF16).astype(F32), U32)
    hi_bits = pltpu.bitcast(hi.astype(BF16).astype(F32), U32)
    return hi_bits | (lo_bits >> 16)


def _unpack_bf16_pair(p):
    lo = pltpu.bitcast(p << 16, F32).astype(BF16)
    hi = pltpu.bitcast(p & jnp.uint32(0xFFFF0000), F32).astype(BF16)
    return lo, hi


def _mix_body(x_ref, mod_ref, mixg_ref, win_ref, lng_ref, lnb_ref, ws_ref, bs_ref, pw_ref, pscale_ref,
              gog_ref, pog_ref, wout_ref, ffng_ref, rw_ref, rb_ref,
              x1_ref, h2p_ref, logit_ref,
              pe_ref, ab_ref, cat_ref, *, tiles_per_seq):
    tr = x_ref.shape[0]
    d = D_MODEL
    seq_tile = pl.program_id(0) % tiles_per_seq
    x = x_ref[...]
    shift_m = mod_ref[:, 0 * d:1 * d]
    scale_m = mod_ref[:, 1 * d:2 * d]
    gate_m = mod_ref[:, 2 * d:3 * d]
    shift_f = mod_ref[:, 3 * d:4 * d]
    scale_f = mod_ref[:, 4 * d:5 * d]

    hb = (_rms(x, mixg_ref[...]) * (1.0 + scale_m) + shift_m).astype(BF16)

    u = _gelu(jnp.dot(hb, win_ref[:, 0:GMLP_WIDTH], preferred_element_type=F32))
    v = _gelu(jnp.dot(hb, win_ref[:, GMLP_WIDTH:2 * GMLP_WIDTH], preferred_element_type=F32))
    slab = 2 * GMLP_BLOCK
    ri = lax.broadcasted_iota(I32, (slab, slab), 0)
    ci = lax.broadcasted_iota(I32, (slab, slab), 1)
    same_block = (ri // GMLP_BLOCK) == (ci // GMLP_BLOCK)
    causal = ((ri % GMLP_BLOCK) // CHUNK) >= ((ci % GMLP_BLOCK) // CHUNK)
    keep = same_block & causal
    for h in range(N_HEADS):
        sl = slice(h * HEAD_DIM, (h + 1) * HEAD_DIM)
        vh = v[:, sl]
        dv = vh - jnp.mean(vh, axis=-1, keepdims=True)
        var = jnp.mean(dv * dv, axis=-1, keepdims=True)
        vn = (dv * lax.rsqrt(var + EPS) * lng_ref[:, sl] + lnb_ref[:, sl]).astype(BF16)
        w_sp = jnp.where(keep, ws_ref[h], 0.0).astype(BF16)
        for s in range(tr // slab):
            rows = slice(s * slab, (s + 1) * slab)
            mixed = jnp.dot(w_sp, vn[rows], preferred_element_type=F32) + bs_ref[h]
            a = u[rows, sl] * mixed
            ab_ref[rows, sl] = a
    a_all = ab_ref[...]
    ssq_a = jnp.sum(a_all * a_all, axis=-1, keepdims=True)
    cat_ref[:, 0:GMLP_WIDTH] = (a_all * lax.rsqrt(ssq_a / GMLP_WIDTH + EPS) * gog_ref[...]).astype(BF16)

    p = jnp.dot(hb, win_ref[:, 2 * GMLP_WIDTH:], preferred_element_type=F32)

    @pl.when(seq_tile == 0)
    def _():
        pe_ref[0:POOL_HALO, :] = jnp.zeros((POOL_HALO, POOL_WIDTH), F32)

    pe_ref[POOL_HALO:, :] = p
    pos1 = (seq_tile * tr + lax.broadcasted_iota(I32, (tr, 1), 0) + 1).astype(F32)
    for g, w in enumerate(POOL_WINDOWS):
        cs = slice(g * POOL_GROUP_DIM, (g + 1) * POOL_GROUP_DIM)
        e = pe_ref[:, cs]
        s = e
        shift = 1
        while shift < w:
            s = s + pltpu.roll(s, shift, 0)
            shift *= 2
        inv = 1.0 / jnp.minimum(pos1, float(w))
        pooled = s[POOL_HALO:] * inv - e[POOL_HALO:]
        y = jnp.dot(pooled.astype(BF16), pw_ref[g], preferred_element_type=F32) * pscale_ref[:, cs]
        ab_ref[:, cs] = y
    pe_ref[0:POOL_HALO, :] = pe_ref[tr:tr + POOL_HALO, :]
    b_all = ab_ref[...]
    ssq_b = jnp.sum(b_all * b_all, axis=-1, keepdims=True)
    cat_ref[:, GMLP_WIDTH:] = (b_all * lax.rsqrt(ssq_b / POOL_WIDTH + EPS) * pog_ref[...]).astype(BF16)

    x1 = x + gate_m * jnp.dot(cat_ref[...], wout_ref[...], preferred_element_type=F32)
    x1_ref[...] = x1
    h2 = _rms(x1, ffng_ref[...]) * (1.0 + scale_f) + shift_f
    h2p_ref[...] = _pack_bf16_pair(h2[:, :HALF], h2[:, HALF:])
    h_hi = h2.astype(BF16)
    h_lo = (h2 - h_hi.astype(F32)).astype(BF16)
    l_hi = jnp.dot(h_hi, rw_ref[...], preferred_element_type=F32)
    l_lo = jnp.dot(h_lo, rw_ref[:, 0:LANES], preferred_element_type=F32)
    logit_ref[...] = l_hi[:, 0:LANES] + l_hi[:, LANES:] + l_lo + rb_ref[...]


def _const_spec(shape):
    nd = len(shape)
    return pl.BlockSpec(shape, lambda i: (0,) * nd, pipeline_mode=pl.Buffered(1))


def _mix(x2d, mod3, seq, mix_g, w_in, ln_g, ln_b, ws2, bs2, pool_w, pool_scale, go_g, po_g, w_out, ffn_g, rw, rb):
    t, d = x2d.shape
    tr = MIX_ROWS
    tps = seq // tr
    row_spec = lambda cols: pl.BlockSpec((tr, cols), lambda i: (i, 0))
    consts = [mix_g, w_in, ln_g, ln_b, ws2, bs2, pool_w, pool_scale, go_g, po_g, w_out, ffn_g, rw, rb]
    return pl.pallas_call(
        functools.partial(_mix_body, tiles_per_seq=tps),
        grid=(t // tr,),
        in_specs=[row_spec(d),
                  pl.BlockSpec((None, 1, mod3.shape[2]), lambda i: (i // tps, 0, 0))]
                 + [_const_spec(a.shape) for a in consts],
        out_specs=[row_spec(d), row_spec(HALF), row_spec(LANES)],
        out_shape=[jax.ShapeDtypeStruct((t, d), F32),
                   jax.ShapeDtypeStruct((t, HALF), U32),
                   jax.ShapeDtypeStruct((t, LANES), F32)],
        scratch_shapes=[pltpu.VMEM((tr + POOL_HALO, POOL_WIDTH), F32),
                        pltpu.VMEM((tr, GMLP_WIDTH), F32),
                        pltpu.VMEM((tr, d), BF16)],
        compiler_params=_params(1),
        name="mix",
    )(x2d, mod3, *consts)


def _route_body(lg_ref, w2_ref, dest_ref, gate_ref, cnt_ref, w2b_ref, tot_ref, run_ref, start_ref):
    phase = pl.program_id(0)
    i = pl.program_id(1)
    tt = lg_ref.shape[0]
    w2b_ref[...] = w2_ref[...].astype(BF16)
    lane = lax.broadcasted_iota(I32, (tt, LANES), 1)
    l = jnp.where(lane < N_EXPERTS, lg_ref[...], -jnp.inf)
    sels, vals = [], []
    for _ in range(TOP_K):
        m = jnp.max(l, axis=1, keepdims=True)
        idx = jnp.min(jnp.where(l == m, lane, LANES), axis=1, keepdims=True)
        sel = lane == idx
        sels.append(sel)
        vals.append(m)
        l = jnp.where(sel, -jnp.inf, l)
    onehot = sels[0].astype(F32)
    for sel in sels[1:]:
        onehot = onehot + sel.astype(F32)
    colsum = jnp.sum(onehot, axis=0, keepdims=True)

    @pl.when((phase == 0) & (i == 0))
    def _():
        tot_ref[...] = jnp.zeros_like(tot_ref)

    @pl.when(phase == 0)
    def _():
        tot_ref[...] += colsum

    @pl.when((phase == 1) & (i == 0))
    def _():
        tot = tot_ref[...]
        padded = jnp.floor((tot + (SUB_ROWS - 1)) / SUB_ROWS) * SUB_ROWS
        r = lax.broadcasted_iota(I32, (LANES, LANES), 0)
        c = lax.broadcasted_iota(I32, (LANES, LANES), 1)
        col = jnp.sum(jnp.where(r == c, jnp.broadcast_to(padded, (LANES, LANES)), 0.0), axis=1, keepdims=True)
        start_ref[...] = jnp.sum(jnp.where(r < c, col, 0.0), axis=0, keepdims=True)
        run_ref[...] = jnp.zeros_like(run_ref)
        cnt_ref[...] = tot

    @pl.when(phase == 1)
    def _():
        r = lax.broadcasted_iota(I32, (tt, tt), 0)
        c = lax.broadcasted_iota(I32, (tt, tt), 1)
        earlier = (r > c).astype(BF16)
        prefix = jnp.dot(earlier, onehot.astype(BF16), preferred_element_type=F32)
        base = prefix + run_ref[...] + start_ref[...]
        dest = [jnp.sum(jnp.where(sel, base, 0.0), axis=1, keepdims=True) for sel in sels]
        dest_ref[...] = jnp.concatenate(dest, axis=1).astype(I32)
        ex = [jnp.exp(vk - vals[0]) for vk in vals]
        den = ex[0] + ex[1] + ex[2] + ex[3]
        gate_ref[...] = jnp.concatenate([e / den for e in ex], axis=1)
        run_ref[...] += colsum


def _route(logits, w2):
    t = logits.shape[0]
    tt = ROUTE_ROWS
    n_i = t // tt
    n_e, f, d = w2.shape
    slabs, rem = divmod(2 * n_i, n_e)
    assert rem == 0 and f % slabs == 0, "route steps must split the expert weights evenly"
    slab_spec = pl.BlockSpec((None, f // slabs, d), lambda p, i: ((p * n_i + i) // slabs, (p * n_i + i) % slabs, 0))
    return pl.pallas_call(
        _route_body,
        grid=(2, n_i),
        in_specs=[pl.BlockSpec((tt, LANES), lambda p, i: (i, 0)), slab_spec],
        out_specs=[pl.BlockSpec((tt, TOP_K), lambda p, i: (i * p, 0)),
                   pl.BlockSpec((tt, TOP_K), lambda p, i: (i * p, 0)),
                   pl.BlockSpec((1, LANES), lambda p, i: (0, 0)),
                   slab_spec],
        out_shape=[jax.ShapeDtypeStruct((t, TOP_K), I32),
                   jax.ShapeDtypeStruct((t, TOP_K), F32),
                   jax.ShapeDtypeStruct((1, LANES), F32),
                   jax.ShapeDtypeStruct(w2.shape, BF16)],
        scratch_shapes=[pltpu.VMEM((1, LANES), F32)] * 3,
        compiler_params=_params(2),
        name="route",
    )(logits, w2)


def _dispatch_body(dest_sm, padfirst_sm, npad_sm, h2p_ref, w1g_ref, w1l_ref, rows_ref, w1r_ref, zero_ref, sem, zsem):
    td = h2p_ref.shape[0]
    step = pl.program_id(0)
    base = step * (td * TOP_K)
    w1r_ref[:, 0:F_CHUNK] = w1g_ref[...].astype(BF16)
    w1r_ref[:, F_CHUNK:] = w1l_ref[...].astype(BF16)

    def issue(g, carry):
        t0 = pl.multiple_of(g * SUBLANES, SUBLANES)
        first = base + t0 * TOP_K
        for i in range(SUBLANES):
            for k in range(TOP_K):
                d = dest_sm[first + (i * TOP_K + k)]
                pltpu.make_async_copy(h2p_ref.at[pl.ds(t0 + i, 1), :], rows_ref.at[pl.ds(d, 1), :],
                                      sem).start(priority=k % 2)
        return carry

    lax.fori_loop(0, td // SUBLANES, issue, 0)

    @pl.when(step == 0)
    def _():
        zero_ref[...] = jnp.zeros_like(zero_ref)

        def zero_copy(r):
            return pltpu.make_async_copy(zero_ref, rows_ref.at[pl.ds(r, 1), :], zsem)

        def per_range(e, carry):
            first = padfirst_sm[e]
            n_pad = npad_sm[e]

            def start(r, c):
                zero_copy(first + r).start()
                return c

            def wait(r, c):
                zero_copy(first + r).wait()
                return c

            lax.fori_loop(0, n_pad, start, 0)
            lax.fori_loop(0, n_pad, wait, 0)
            return carry

        lax.fori_loop(0, N_EXPERTS + 1, per_range, 0)

    for k in range(TOP_K):
        pltpu.make_async_copy(h2p_ref, rows_ref.at[pl.ds(0, td), :], sem).wait()


def _dispatch(dest_flat, pad_first, n_pad, h2p, w1, n_rows):
    t = h2p.shape[0]
    td = DISPATCH_ROWS
    n_e, d, f2 = w1.shape
    nj = N_FCHUNKS
    assert t // td == n_e * nj and f2 == 2 * nj * F_CHUNK, "one up-projection chunk per dispatch step"
    return pl.pallas_call(
        _dispatch_body,
        grid_spec=pltpu.PrefetchScalarGridSpec(
            num_scalar_prefetch=3,
            grid=(t // td,),
            in_specs=[pl.BlockSpec((td, HALF), lambda i, *_: (i, 0)),
                      pl.BlockSpec((None, d, F_CHUNK), lambda i, *_: (i // nj, 0, i % nj)),
                      pl.BlockSpec((None, d, F_CHUNK), lambda i, *_: (i // nj, 0, nj + i % nj))],
            out_specs=[pl.BlockSpec(memory_space=pl.ANY),
                       pl.BlockSpec((None, None, d, 2 * F_CHUNK), lambda i, *_: (i // nj, i % nj, 0, 0))],
            scratch_shapes=[pltpu.VMEM((1, HALF), U32), pltpu.SemaphoreType.DMA, pltpu.SemaphoreType.DMA],
        ),
        out_shape=[jax.ShapeDtypeStruct((n_rows, HALF), U32),
                   jax.ShapeDtypeStruct((n_e, nj, d, 2 * F_CHUNK), BF16)],
        compiler_params=_params(1),
        name="dispatch",
    )(dest_flat, pad_first, n_pad, h2p, w1, w1)


def _expert_body(vt, ve, vlo, vhi, vfirst, x_ref, w1_ref, b1g_ref, b1l_ref, w2_ref, b2_ref, o_ref):
    del vt, ve
    v = pl.program_id(0)
    j = pl.program_id(1)
    lo = vlo[v]
    hi = vhi[v]

    @pl.when((j == 0) & (vfirst[v] == 1))
    def _():
        o_ref[...] = jnp.zeros_like(o_ref)

    @pl.when(hi > lo)
    def _():
        b2_first = jnp.where(j == 0, b2_ref[...], 0.0)

        def run(first_sub, n_sub):
            rows = pl.ds(pl.multiple_of(first_sub * SUB_ROWS, SUB_ROWS), n_sub * SUB_ROWS)
            x_lo, x_hi = _unpack_bf16_pair(x_ref[rows, :])
            a = (jnp.dot(x_lo, w1_ref[0:HALF, :], preferred_element_type=F32)
                 + jnp.dot(x_hi, w1_ref[HALF:, :], preferred_element_type=F32))
            glu = jnp.minimum(a[:, 0:F_CHUNK] + b1g_ref[...], SWIGLU_LIMIT)
            lin = jnp.clip(a[:, F_CHUNK:] + b1l_ref[...], -SWIGLU_LIMIT, SWIGLU_LIMIT)
            act = glu * jax.nn.sigmoid(SWIGLU_ALPHA * glu) * (lin + 1.0)
            o_ref[rows, :] += jnp.dot(act.astype(BF16), w2_ref[...], preferred_element_type=F32) + b2_first

        n_sub = hi - lo

        @pl.when(n_sub == TILE_SUBS)
        def _():
            run(0, TILE_SUBS // 2)
            run(TILE_SUBS // 2, TILE_SUBS // 2)

        @pl.when(n_sub < TILE_SUBS)
        def _():
            def pair(p, carry):
                run(lo + 2 * p, 2)
                return carry

            lax.fori_loop(0, lax.shift_right_logical(n_sub, 1), pair, 0)

            @pl.when((n_sub & 1) == 1)
            def _():
                run(hi - 1, 1)


def _experts(tables, rows, w1r, b1, w2b, b2):
    n_rows = rows.shape[0]
    n_visits = tables[0].shape[0]
    nj = N_FCHUNKS

    def jeff(j, vlo, vhi, v):
        return jnp.where(vhi[v] > vlo[v], j, nj - 1)

    return pl.pallas_call(
        _expert_body,
        grid_spec=pltpu.PrefetchScalarGridSpec(
            num_scalar_prefetch=5,
            grid=(n_visits, nj),
            in_specs=[
                pl.BlockSpec((TILE_ROWS, HALF), lambda v, j, vt, ve, vlo, vhi, vf: (vt[v], 0)),
                pl.BlockSpec((None, None, D_MODEL, 2 * F_CHUNK),
                             lambda v, j, vt, ve, vlo, vhi, vf: (ve[v], jeff(j, vlo, vhi, v), 0, 0)),
                pl.BlockSpec((None, 1, F_CHUNK),
                             lambda v, j, vt, ve, vlo, vhi, vf: (ve[v], 0, jeff(j, vlo, vhi, v))),
                pl.BlockSpec((None, 1, F_CHUNK),
                             lambda v, j, vt, ve, vlo, vhi, vf: (ve[v], 0, nj + jeff(j, vlo, vhi, v))),
                pl.BlockSpec((None, F_CHUNK, D_MODEL),
                             lambda v, j, vt, ve, vlo, vhi, vf: (ve[v], jeff(j, vlo, vhi, v), 0)),
                pl.BlockSpec((None, 1, D_MODEL), lambda v, j, vt, ve, vlo, vhi, vf: (ve[v], 0, 0)),
            ],
            out_specs=pl.BlockSpec((TILE_ROWS, D_MODEL), lambda v, j, vt, ve, vlo, vhi, vf: (vt[v], 0)),
        ),
        out_shape=jax.ShapeDtypeStruct((n_rows, D_MODEL), F32),
        compiler_params=_params(2),
        name="experts",
    )(*tables, rows, w1r, b1, b1, w2b, b2)


def _layout_tables(counts, n_tiles):
    nblk = (counts + (SUB_ROWS - 1)) // SUB_ROWS
    blk_end = jnp.cumsum(nblk)
    blk_start = blk_end - nblk
    used_rows = blk_end[-1:] * SUB_ROWS
    pad_first = jnp.concatenate([blk_start * SUB_ROWS + counts, used_rows])
    n_pad = jnp.concatenate([nblk * SUB_ROWS - counts, n_tiles * TILE_ROWS - used_rows])

    n_visits = n_tiles + N_EXPERTS
    first_tile = blk_start // TILE_SUBS
    last_tile = (blk_end - 1) // TILE_SUBS
    nvis = jnp.where(nblk > 0, last_tile - first_tile + 1, 0)
    vis_end = jnp.cumsum(nvis)
    vis_start = vis_end - nvis
    total = vis_end[-1]
    v = jnp.arange(n_visits, dtype=I32)
    valid = v < total
    v_eff = jnp.minimum(v, total - 1)
    e = jnp.minimum(jnp.sum((vis_end[None, :] <= v_eff[:, None]).astype(I32), axis=1), N_EXPERTS - 1)
    onehot = (e[:, None] == jnp.arange(N_EXPERTS, dtype=I32)[None, :]).astype(I32)
    pick = lambda a: jnp.sum(onehot * a[None, :], axis=1)
    tile = jnp.minimum(pick(first_tile) + (v_eff - pick(vis_start)) + (v - v_eff), n_tiles - 1)
    lo = jnp.maximum(pick(blk_start), tile * TILE_SUBS) - tile * TILE_SUBS
    hi = jnp.minimum(pick(blk_end), (tile + 1) * TILE_SUBS) - tile * TILE_SUBS
    lo = jnp.where(valid, lo, 0)
    hi = jnp.where(valid, hi, 0)
    prev_tile = jnp.concatenate([jnp.full((1,), -1, I32), tile[:-1]])
    first = (tile != prev_tile).astype(I32)
    visit = tuple(a.astype(I32) for a in (tile, e, lo, hi, first))
    return pad_first.astype(I32), n_pad.astype(I32), visit


def _combine_body(dest_sm, rows_ref, x1_ref, gate_ref, mod_ref, fg_ref, o_ref, buf, sem, *, n_tiles):
    tc = x1_ref.shape[0]
    i = pl.program_id(0)

    def row_copy(d, slot, k, t):
        return pltpu.make_async_copy(rows_ref.at[pl.ds(d, 1), :], buf.at[slot, k, pl.ds(t, 1), :], sem.at[slot])

    def issue(tile, slot):
        base = tile * (tc * TOP_K)

        def one(g, carry):
            t0 = pl.multiple_of(g * SUBLANES, SUBLANES)
            first = base + t0 * TOP_K
            for i in range(SUBLANES):
                for k in range(TOP_K):
                    row_copy(dest_sm[first + (i * TOP_K + k)], slot, k, t0 + i).start(priority=k % 2)
            return carry

        lax.fori_loop(0, tc // SUBLANES, one, 0)

    @pl.when(i == 0)
    def _():
        issue(0, 0)

    @pl.when(i + 1 < n_tiles)
    def _():
        issue(i + 1, (i + 1) % 2)

    slot = i % 2

    for k in range(TOP_K):
        pltpu.make_async_copy(rows_ref.at[pl.ds(0, tc), :], buf.at[slot, k], sem.at[slot]).wait()

    g = gate_ref[...]
    y = g[:, 0:1] * buf[slot, 0]
    for k in range(1, TOP_K):
        y = y + g[:, k:k + 1] * buf[slot, k]
    gate_f = mod_ref[:, 5 * D_MODEL:6 * D_MODEL]
    o_ref[...] = _rms(x1_ref[...] + gate_f * y, fg_ref[...])


def _combine(dest_flat, out_rows, x1, gates, mod3, seq, final_g):
    t, d = x1.shape
    tc = COMBINE_ROWS
    tps = seq // tc
    n_tiles = t // tc
    return pl.pallas_call(
        functools.partial(_combine_body, n_tiles=n_tiles),
        grid_spec=pltpu.PrefetchScalarGridSpec(
            num_scalar_prefetch=1,
            grid=(n_tiles,),
            in_specs=[pl.BlockSpec(memory_space=pl.ANY),
                      pl.BlockSpec((tc, d), lambda i, s: (i, 0)),
                      pl.BlockSpec((tc, TOP_K), lambda i, s: (i, 0)),
                      pl.BlockSpec((None, 1, mod3.shape[2]), lambda i, s: (i // tps, 0, 0)),
                      pl.BlockSpec((1, d), lambda i, s: (0, 0))],
            out_specs=pl.BlockSpec((tc, d), lambda i, s: (i, 0)),
            scratch_shapes=[pltpu.VMEM((2, TOP_K, tc, d), F32),
                            pltpu.SemaphoreType.DMA((2,))],
        ),
        out_shape=jax.ShapeDtypeStruct((t, d), F32),
        compiler_params=_params(1),
        name="combine",
    )(dest_flat, out_rows, x1, gates, mod3, final_g)


def kernel(x, c, mix_norm_g, w_ada, b_ada, w_in, gmlp_ln_g, gmlp_ln_b, gmlp_ws, gmlp_bs, pool_w, pool_scale,
           gmlp_out_g, pool_out_g, w_out, ffn_norm_g, router_w, router_b, moe_w1, moe_b1, moe_w2, moe_b2,
           final_norm_g):
    bsz, seq, d = x.shape
    t = bsz * seq
    assert d == D_MODEL and w_ada.shape[0] == 1, "single-layer block with d_model 2048"
    assert seq % MIX_ROWS == 0 and seq % COMBINE_ROWS == 0 and t % ROUTE_ROWS == 0 and t % DISPATCH_ROWS == 0
    row = lambda a: a.reshape(1, -1)

    c_pad = jnp.zeros((8, d), F32).at[:bsz].set(c)
    mod3 = _ada(c_pad, w_ada[0], row(b_ada[0]))[:bsz].reshape(bsz, 1, 6 * d)

    ws2 = jnp.tile(gmlp_ws[0], (1, 2, 2))
    bs2 = jnp.tile(gmlp_bs[0], (1, 2))[:, :, None]
    rw_hi = router_w[0].astype(BF16)
    rw_lo = (router_w[0] - rw_hi.astype(F32)).astype(BF16)
    lane_pad = lambda a: jnp.pad(a, ((0, 0), (0, LANES - a.shape[1])))
    rw = jnp.concatenate([lane_pad(rw_hi), lane_pad(rw_lo)], axis=1)
    rb = lane_pad(row(router_b[0]))

    x1, h2p, logits = _mix(
        x.reshape(t, d), mod3, seq, row(mix_norm_g[0]), w_in[0].astype(BF16), row(gmlp_ln_g[0]), row(gmlp_ln_b[0]),
        ws2, bs2, pool_w[0].astype(BF16), row(pool_scale[0]), row(gmlp_out_g[0]), row(pool_out_g[0]),
        w_out[0].astype(BF16), row(ffn_norm_g[0]), rw, rb)

    dest, gates, counts, w2b = _route(logits, moe_w2[0])
    dest_flat = dest.reshape(t * TOP_K)

    n_asg = t * TOP_K
    n_rows = -(-(n_asg + N_EXPERTS * SUB_ROWS) // TILE_ROWS) * TILE_ROWS
    pad_first, n_pad, tables = _layout_tables(counts[0, :N_EXPERTS].astype(I32), n_rows // TILE_ROWS)
    rows, w1r = _dispatch(dest_flat, pad_first, n_pad, h2p, moe_w1[0], n_rows)
    e, f = N_EXPERTS, EXPERT_DIM
    out_rows = _experts(tables, rows, w1r, moe_b1[0].reshape(e, 1, 2 * f), w2b, moe_b2[0].reshape(e, 1, d))

    y = _combine(dest_flat, out_rows, x1, gates, mod3, seq, row(final_norm_g))
    return y.reshape(bsz, seq, d)
```

```python
import functools

import jax
import jax.numpy as jnp
from jax import lax
from jax.experimental import pallas as pl
from jax.experimental.pallas import tpu as pltpu

F32 = jnp.float32
BF16 = jnp.bfloat16
I32 = jnp.int32
U32 = jnp.uint32

D_MODEL = 2048
GMLP_WIDTH = 1024
HEAD_DIM = 128
N_HEADS = GMLP_WIDTH // HEAD_DIM
GMLP_BLOCK = 128
CHUNK = 64
POOL_WIDTH = 1024
POOL_WINDOWS = (2, 4, 8, 16)
POOL_GROUP_DIM = POOL_WIDTH // len(POOL_WINDOWS)
POOL_HALO = 16
N_EXPERTS = 32
TOP_K = 4
EXPERT_DIM = 2048
SWIGLU_ALPHA = 1.702
SWIGLU_LIMIT = 7.0
EPS = 1e-5

LANES = 128
SUBLANES = 8
SUB_ROWS = 256
TILE_SUBS = 4
TILE_ROWS = SUB_ROWS * TILE_SUBS
F_CHUNK = 512
N_FCHUNKS = EXPERT_DIM // F_CHUNK
HALF = D_MODEL // 2

MIX_ROWS = 256
ROUTE_ROWS = 512
DISPATCH_ROWS = 256
COMBINE_ROWS = 256
ADA_COLS = 1024

VMEM_LIMIT = 56 * 1024 * 1024


def _params(n_axes, vmem=VMEM_LIMIT):
    return pltpu.CompilerParams(dimension_semantics=("arbitrary",) * n_axes, vmem_limit_bytes=vmem)


def _rms(x, g):
    return x * lax.rsqrt(jnp.mean(x * x, axis=-1, keepdims=True) + EPS) * g


def _gelu(x):
    return 0.5 * x * (1.0 + lax.erf(x * (2.0 ** -0.5)))


def _ada_body(c_ref, w_ref, b_ref, o_ref):
    c = c_ref[...]
    cond = c * jax.nn.sigmoid(c)
    o_ref[...] = jnp.dot(cond.astype(BF16), w_ref[...].astype(BF16), preferred_element_type=F32) + b_ref[...]


def _ada(c_pad, w_ada, b_ada):
    rows, d = c_pad.shape
    n = w_ada.shape[1]
    return pl.pallas_call(
        _ada_body,
        grid=(n // ADA_COLS,),
        in_specs=[pl.BlockSpec((rows, d), lambda j: (0, 0)),
                  pl.BlockSpec((d, ADA_COLS), lambda j: (0, j)),
                  pl.BlockSpec((1, ADA_COLS), lambda j: (0, j))],
        out_specs=pl.BlockSpec((rows, ADA_COLS), lambda j: (0, j)),
        out_shape=jax.ShapeDtypeStruct((rows, n), F32),
        compiler_params=_params(1),
        name="ada",
    )(c_pad, w_ada, b_ada)


def _pack_bf16_pair(lo, hi):
    lo_bits = pltpu.bitcast(lo.astype(BF16).astype(F32), U32)
    hi_bits = pltpu.bitcast(hi.astype(BF16).astype(F32), U32)
    return hi_bits | (lo_bits >> 16)


def _unpack_bf16_pair(p):
    lo = pltpu.bitcast(p << 16, F32).astype(BF16)
    hi = pltpu.bitcast(p & jnp.uint32(0xFFFF0000), F32).astype(BF16)
    return lo, hi


def _mix_body(x_ref, mod_ref, w2_ref, mixg_ref, win_ref, lng_ref, lnb_ref, ws_ref, bs_ref, pw_ref, pscale_ref,
              gog_ref, pog_ref, wout_ref, ffng_ref, rw_ref, rb_ref,
              x1_ref, h2p_ref, logit_ref, w2b_ref,
              pe_ref, ab_ref, cat_ref, *, tiles_per_seq):
    tr = x_ref.shape[0]
    w2b_ref[...] = w2_ref[...].astype(BF16)
    d = D_MODEL
    seq_tile = pl.program_id(0) % tiles_per_seq
    x = x_ref[...]
    shift_m = mod_ref[:, 0 * d:1 * d]
    scale_m = mod_ref[:, 1 * d:2 * d]
    gate_m = mod_ref[:, 2 * d:3 * d]
    shift_f = mod_ref[:, 3 * d:4 * d]
    scale_f = mod_ref[:, 4 * d:5 * d]

    hb = (_rms(x, mixg_ref[...]) * (1.0 + scale_m) + shift_m).astype(BF16)

    u = _gelu(jnp.dot(hb, win_ref[:, 0:GMLP_WIDTH], preferred_element_type=F32))
    v = _gelu(jnp.dot(hb, win_ref[:, GMLP_WIDTH:2 * GMLP_WIDTH], preferred_element_type=F32))
    slab = 2 * GMLP_BLOCK
    ri = lax.broadcasted_iota(I32, (slab, slab), 0)
    ci = lax.broadcasted_iota(I32, (slab, slab), 1)
    same_block = (ri // GMLP_BLOCK) == (ci // GMLP_BLOCK)
    causal = ((ri % GMLP_BLOCK) // CHUNK) >= ((ci % GMLP_BLOCK) // CHUNK)
    keep = same_block & causal
    for h in range(N_HEADS):
        sl = slice(h * HEAD_DIM, (h + 1) * HEAD_DIM)
        vh = v[:, sl]
        dv = vh - jnp.mean(vh, axis=-1, keepdims=True)
        var = jnp.mean(dv * dv, axis=-1, keepdims=True)
        vn = (dv * lax.rsqrt(var + EPS) * lng_ref[:, sl] + lnb_ref[:, sl]).astype(BF16)
        w_sp = jnp.where(keep, ws_ref[h], 0.0).astype(BF16)
        for s in range(tr // slab):
            rows = slice(s * slab, (s + 1) * slab)
            mixed = jnp.dot(w_sp, vn[rows], preferred_element_type=F32) + bs_ref[h]
            a = u[rows, sl] * mixed
            ab_ref[rows, sl] = a
    a_all = ab_ref[...]
    ssq_a = jnp.sum(a_all * a_all, axis=-1, keepdims=True)
    cat_ref[:, 0:GMLP_WIDTH] = (a_all * lax.rsqrt(ssq_a / GMLP_WIDTH + EPS) * gog_ref[...]).astype(BF16)

    p = jnp.dot(hb, win_ref[:, 2 * GMLP_WIDTH:], preferred_element_type=F32)

    @pl.when(seq_tile == 0)
    def _():
        pe_ref[0:POOL_HALO, :] = jnp.zeros((POOL_HALO, POOL_WIDTH), F32)

    pe_ref[POOL_HALO:, :] = p
    pos1 = (seq_tile * tr + lax.broadcasted_iota(I32, (tr, 1), 0) + 1).astype(F32)
    for g, w in enumerate(POOL_WINDOWS):
        cs = slice(g * POOL_GROUP_DIM, (g + 1) * POOL_GROUP_DIM)
        e = pe_ref[:, cs]
        s = e
        shift = 1
        while shift < w:
            s = s + pltpu.roll(s, shift, 0)
            shift *= 2
        inv = 1.0 / jnp.minimum(pos1, float(w))
        pooled = s[POOL_HALO:] * inv - e[POOL_HALO:]
        y = jnp.dot(pooled.astype(BF16), pw_ref[g], preferred_element_type=F32) * pscale_ref[:, cs]
        ab_ref[:, cs] = y
    pe_ref[0:POOL_HALO, :] = pe_ref[tr:tr + POOL_HALO, :]
    b_all = ab_ref[...]
    ssq_b = jnp.sum(b_all * b_all, axis=-1, keepdims=True)
    cat_ref[:, GMLP_WIDTH:] = (b_all * lax.rsqrt(ssq_b / POOL_WIDTH + EPS) * pog_ref[...]).astype(BF16)

    x1 = x + gate_m * jnp.dot(cat_ref[...], wout_ref[...], preferred_element_type=F32)
    x1_ref[...] = x1
    h2 = _rms(x1, ffng_ref[...]) * (1.0 + scale_f) + shift_f
    h2p_ref[...] = _pack_bf16_pair(h2[:, :HALF], h2[:, HALF:])
    h_hi = h2.astype(BF16)
    h_lo = (h2 - h_hi.astype(F32)).astype(BF16)
    l_hi = jnp.dot(h_hi, rw_ref[...], preferred_element_type=F32)
    l_lo = jnp.dot(h_lo, rw_ref[:, 0:LANES], preferred_element_type=F32)
    logit_ref[...] = l_hi[:, 0:LANES] + l_hi[:, LANES:] + l_lo + rb_ref[...]


def _const_spec(shape):
    nd = len(shape)
    return pl.BlockSpec(shape, lambda i: (0,) * nd, pipeline_mode=pl.Buffered(1))


def _mix(x2d, mod3, seq, w2, mix_g, w_in, ln_g, ln_b, ws2, bs2, pool_w, pool_scale, go_g, po_g, w_out, ffn_g, rw, rb):
    t, d = x2d.shape
    tr = MIX_ROWS
    tps = seq // tr
    n_e, f, _ = w2.shape
    slabs, rem = divmod(t // tr, n_e)
    assert rem == 0 and f % slabs == 0, "mix steps must split the expert weights evenly"
    slab_spec = pl.BlockSpec((None, f // slabs, d), lambda i: (i // slabs, i % slabs, 0))
    row_spec = lambda cols: pl.BlockSpec((tr, cols), lambda i: (i, 0))
    consts = [mix_g, w_in, ln_g, ln_b, ws2, bs2, pool_w, pool_scale, go_g, po_g, w_out, ffn_g, rw, rb]
    return pl.pallas_call(
        functools.partial(_mix_body, tiles_per_seq=tps),
        grid=(t // tr,),
        in_specs=[row_spec(d),
                  pl.BlockSpec((None, 1, mod3.shape[2]), lambda i: (i // tps, 0, 0)),
                  slab_spec]
                 + [_const_spec(a.shape) for a in consts],
        out_specs=[row_spec(d), row_spec(HALF), row_spec(LANES), slab_spec],
        out_shape=[jax.ShapeDtypeStruct((t, d), F32),
                   jax.ShapeDtypeStruct((t, HALF), U32),
                   jax.ShapeDtypeStruct((t, LANES), F32),
                   jax.ShapeDtypeStruct(w2.shape, BF16)],
        scratch_shapes=[pltpu.VMEM((tr + POOL_HALO, POOL_WIDTH), F32),
                        pltpu.VMEM((tr, GMLP_WIDTH), F32),
                        pltpu.VMEM((tr, d), BF16)],
        compiler_params=_params(1),
        name="mix",
    )(x2d, mod3, w2, *consts)


def _route_body(lg_ref, w1g_ref, dest_ref, gate_ref, cnt_ref, w1gb_ref, tot_ref, run_ref, start_ref):
    phase = pl.program_id(0)
    i = pl.program_id(1)
    tt = lg_ref.shape[0]
    w1gb_ref[...] = w1g_ref[...].astype(BF16)
    lane = lax.broadcasted_iota(I32, (tt, LANES), 1)
    l = jnp.where(lane < N_EXPERTS, lg_ref[...], -jnp.inf)
    sels, vals = [], []
    for _ in range(TOP_K):
        m = jnp.max(l, axis=1, keepdims=True)
        idx = jnp.min(jnp.where(l == m, lane, LANES), axis=1, keepdims=True)
        sel = lane == idx
        sels.append(sel)
        vals.append(m)
        l = jnp.where(sel, -jnp.inf, l)
    onehot = sels[0].astype(F32)
    for sel in sels[1:]:
        onehot = onehot + sel.astype(F32)
    colsum = jnp.sum(onehot, axis=0, keepdims=True)

    @pl.when((phase == 0) & (i == 0))
    def _():
        tot_ref[...] = jnp.zeros_like(tot_ref)

    @pl.when(phase == 0)
    def _():
        tot_ref[...] += colsum

    @pl.when((phase == 1) & (i == 0))
    def _():
        tot = tot_ref[...]
        padded = jnp.floor((tot + (SUB_ROWS - 1)) / SUB_ROWS) * SUB_ROWS
        r = lax.broadcasted_iota(I32, (LANES, LANES), 0)
        c = lax.broadcasted_iota(I32, (LANES, LANES), 1)
        col = jnp.sum(jnp.where(r == c, jnp.broadcast_to(padded, (LANES, LANES)), 0.0), axis=1, keepdims=True)
        start_ref[...] = jnp.sum(jnp.where(r < c, col, 0.0), axis=0, keepdims=True)
        run_ref[...] = jnp.zeros_like(run_ref)
        cnt_ref[...] = tot

    @pl.when(phase == 1)
    def _():
        r = lax.broadcasted_iota(I32, (tt, tt), 0)
        c = lax.broadcasted_iota(I32, (tt, tt), 1)
        earlier = (r > c).astype(BF16)
        prefix = jnp.dot(earlier, onehot.astype(BF16), preferred_element_type=F32)
        base = prefix + run_ref[...] + start_ref[...]
        dest = [jnp.sum(jnp.where(sel, base, 0.0), axis=1, keepdims=True) for sel in sels]
        dest_ref[...] = jnp.concatenate(dest, axis=1).astype(I32)
        ex = [jnp.exp(vk - vals[0]) for vk in vals]
        den = ex[0] + ex[1] + ex[2] + ex[3]
        gate_ref[...] = jnp.concatenate([e / den for e in ex], axis=1)
        run_ref[...] += colsum


def _route(logits, w1):
    t = logits.shape[0]
    tt = ROUTE_ROWS
    n_i = t // tt
    n_e, d, f2 = w1.shape
    nj = N_FCHUNKS
    assert 2 * n_i == n_e * nj and f2 == 2 * nj * F_CHUNK, "one up-projection chunk per route step"
    return pl.pallas_call(
        _route_body,
        grid=(2, n_i),
        in_specs=[pl.BlockSpec((tt, LANES), lambda p, i: (i, 0)),
                  pl.BlockSpec((None, d, F_CHUNK), lambda p, i: ((p * n_i + i) // nj, 0, (p * n_i + i) % nj))],
        out_specs=[pl.BlockSpec((tt, TOP_K), lambda p, i: (i * p, 0)),
                   pl.BlockSpec((tt, TOP_K), lambda p, i: (i * p, 0)),
                   pl.BlockSpec((1, LANES), lambda p, i: (0, 0)),
                   pl.BlockSpec((None, None, d, F_CHUNK),
                                lambda p, i: ((p * n_i + i) // nj, (p * n_i + i) % nj, 0, 0))],
        out_shape=[jax.ShapeDtypeStruct((t, TOP_K), I32),
                   jax.ShapeDtypeStruct((t, TOP_K), F32),
                   jax.ShapeDtypeStruct((1, LANES), F32),
                   jax.ShapeDtypeStruct((n_e, nj, d, F_CHUNK), BF16)],
        scratch_shapes=[pltpu.VMEM((1, LANES), F32)] * 3,
        compiler_params=_params(2),
        name="route",
    )(logits, w1)


def _dispatch_body(dest_sm, padfirst_sm, npad_sm, h2p_ref, w1l_ref, rows_ref, w1lb_ref, zero_ref, sem, zsem):
    td = h2p_ref.shape[0]
    step = pl.program_id(0)
    base = step * (td * TOP_K)
    w1lb_ref[...] = w1l_ref[...].astype(BF16)

    def issue(g, carry):
        t0 = pl.multiple_of(g * SUBLANES, SUBLANES)
        first = base + t0 * TOP_K
        for i in range(SUBLANES):
            for k in range(TOP_K):
                d = dest_sm[first + (i * TOP_K + k)]
                pltpu.make_async_copy(h2p_ref.at[pl.ds(t0 + i, 1), :], rows_ref.at[pl.ds(d, 1), :],
                                      sem).start(priority=k % 2)
        return carry

    lax.fori_loop(0, td // SUBLANES, issue, 0)

    @pl.when(step == 0)
    def _():
        zero_ref[...] = jnp.zeros_like(zero_ref)

        def zero_copy(r):
            return pltpu.make_async_copy(zero_ref, rows_ref.at[pl.ds(r, 1), :], zsem)

        def per_range(e, carry):
            first = padfirst_sm[e]
            n_pad = npad_sm[e]

            def start(r, c):
                zero_copy(first + r).start()
                return c

            def wait(r, c):
                zero_copy(first + r).wait()
                return c

            lax.fori_loop(0, n_pad, start, 0)
            lax.fori_loop(0, n_pad, wait, 0)
            return carry

        lax.fori_loop(0, N_EXPERTS + 1, per_range, 0)

    for k in range(TOP_K):
        pltpu.make_async_copy(h2p_ref, rows_ref.at[pl.ds(0, td), :], sem).wait()


def _dispatch(dest_flat, pad_first, n_pad, h2p, w1, n_rows):
    t = h2p.shape[0]
    td = DISPATCH_ROWS
    n_e, d, f2 = w1.shape
    nj = N_FCHUNKS
    assert t // td == n_e * nj and f2 == 2 * nj * F_CHUNK, "one up-projection chunk per dispatch step"
    return pl.pallas_call(
        _dispatch_body,
        grid_spec=pltpu.PrefetchScalarGridSpec(
            num_scalar_prefetch=3,
            grid=(t // td,),
            in_specs=[pl.BlockSpec((td, HALF), lambda i, *_: (i, 0)),
                      pl.BlockSpec((None, d, F_CHUNK), lambda i, *_: (i // nj, 0, nj + i % nj))],
            out_specs=[pl.BlockSpec(memory_space=pl.ANY),
                       pl.BlockSpec((None, None, d, F_CHUNK), lambda i, *_: (i // nj, i % nj, 0, 0))],
            scratch_shapes=[pltpu.VMEM((1, HALF), U32), pltpu.SemaphoreType.DMA, pltpu.SemaphoreType.DMA],
        ),
        out_shape=[jax.ShapeDtypeStruct((n_rows, HALF), U32),
                   jax.ShapeDtypeStruct((n_e, nj, d, F_CHUNK), BF16)],
        compiler_params=_params(1),
        name="dispatch",
    )(dest_flat, pad_first, n_pad, h2p, w1)


def _expert_body(vt, ve, vlo, vhi, vfirst, x_ref, w1g_ref, w1l_ref, b1g_ref, b1l_ref, w2_ref, b2_ref, o_ref):
    del vt, ve
    v = pl.program_id(0)
    j = pl.program_id(1)
    lo = vlo[v]
    hi = vhi[v]

    @pl.when((j == 0) & (vfirst[v] == 1))
    def _():
        o_ref[...] = jnp.zeros_like(o_ref)

    @pl.when(hi > lo)
    def _():
        b2_first = jnp.where(j == 0, b2_ref[...], 0.0)

        def run(first_sub, n_sub):
            rows = pl.ds(pl.multiple_of(first_sub * SUB_ROWS, SUB_ROWS), n_sub * SUB_ROWS)
            x_lo, x_hi = _unpack_bf16_pair(x_ref[rows, :])
            a_g = (jnp.dot(x_lo, w1g_ref[0:HALF, :], preferred_element_type=F32)
                   + jnp.dot(x_hi, w1g_ref[HALF:, :], preferred_element_type=F32) + b1g_ref[...])
            a_l = (jnp.dot(x_lo, w1l_ref[0:HALF, :], preferred_element_type=F32)
                   + jnp.dot(x_hi, w1l_ref[HALF:, :], preferred_element_type=F32) + b1l_ref[...])
            glu = jnp.minimum(a_g, SWIGLU_LIMIT)
            lin = jnp.clip(a_l, -SWIGLU_LIMIT, SWIGLU_LIMIT)
            act = glu * jax.nn.sigmoid(SWIGLU_ALPHA * glu) * (lin + 1.0)
            o_ref[rows, :] += jnp.dot(act.astype(BF16), w2_ref[...], preferred_element_type=F32) + b2_first

        n_sub = hi - lo

        @pl.when(n_sub == TILE_SUBS)
        def _():
            run(0, TILE_SUBS // 2)
            run(TILE_SUBS // 2, TILE_SUBS // 2)

        @pl.when(n_sub < TILE_SUBS)
        def _():
            def pair(p, carry):
                run(lo + 2 * p, 2)
                return carry

            lax.fori_loop(0, lax.shift_right_logical(n_sub, 1), pair, 0)

            @pl.when((n_sub & 1) == 1)
            def _():
                run(hi - 1, 1)


def _experts(tables, rows, w1gb, w1lb, b1, w2b, b2):
    n_rows = rows.shape[0]
    n_visits = tables[0].shape[0]
    nj = N_FCHUNKS

    def jeff(j, vlo, vhi, v):
        return jnp.where(vhi[v] > vlo[v], j, nj - 1)

    return pl.pallas_call(
        _expert_body,
        grid_spec=pltpu.PrefetchScalarGridSpec(
            num_scalar_prefetch=5,
            grid=(n_visits, nj),
            in_specs=[
                pl.BlockSpec((TILE_ROWS, HALF), lambda v, j, vt, ve, vlo, vhi, vf: (vt[v], 0)),
                pl.BlockSpec((None, None, D_MODEL, F_CHUNK),
                             lambda v, j, vt, ve, vlo, vhi, vf: (ve[v], jeff(j, vlo, vhi, v), 0, 0)),
                pl.BlockSpec((None, None, D_MODEL, F_CHUNK),
                             lambda v, j, vt, ve, vlo, vhi, vf: (ve[v], jeff(j, vlo, vhi, v), 0, 0)),
                pl.BlockSpec((None, 1, F_CHUNK),
                             lambda v, j, vt, ve, vlo, vhi, vf: (ve[v], 0, jeff(j, vlo, vhi, v))),
                pl.BlockSpec((None, 1, F_CHUNK),
                             lambda v, j, vt, ve, vlo, vhi, vf: (ve[v], 0, nj + jeff(j, vlo, vhi, v))),
                pl.BlockSpec((None, F_CHUNK, D_MODEL),
                             lambda v, j, vt, ve, vlo, vhi, vf: (ve[v], jeff(j, vlo, vhi, v), 0)),
                pl.BlockSpec((None, 1, D_MODEL), lambda v, j, vt, ve, vlo, vhi, vf: (ve[v], 0, 0)),
            ],
            out_specs=pl.BlockSpec((TILE_ROWS, D_MODEL), lambda v, j, vt, ve, vlo, vhi, vf: (vt[v], 0)),
        ),
        out_shape=jax.ShapeDtypeStruct((n_rows, D_MODEL), F32),
        compiler_params=_params(2),
        name="experts",
    )(*tables, rows, w1gb, w1lb, b1, b1, w2b, b2)


def _layout_tables(counts, n_tiles):
    nblk = (counts + (SUB_ROWS - 1)) // SUB_ROWS
    blk_end = jnp.cumsum(nblk)
    blk_start = blk_end - nblk
    used_rows = blk_end[-1:] * SUB_ROWS
    pad_first = jnp.concatenate([blk_start * SUB_ROWS + counts, used_rows])
    n_pad = jnp.concatenate([nblk * SUB_ROWS - counts, n_tiles * TILE_ROWS - used_rows])

    n_visits = n_tiles + N_EXPERTS
    first_tile = blk_start // TILE_SUBS
    last_tile = (blk_end - 1) // TILE_SUBS
    nvis = jnp.where(nblk > 0, last_tile - first_tile + 1, 0)
    vis_end = jnp.cumsum(nvis)
    vis_start = vis_end - nvis
    total = vis_end[-1]
    v = jnp.arange(n_visits, dtype=I32)
    valid = v < total
    v_eff = jnp.minimum(v, total - 1)
    e = jnp.minimum(jnp.sum((vis_end[None, :] <= v_eff[:, None]).astype(I32), axis=1), N_EXPERTS - 1)
    onehot = (e[:, None] == jnp.arange(N_EXPERTS, dtype=I32)[None, :]).astype(I32)
    pick = lambda a: jnp.sum(onehot * a[None, :], axis=1)
    tile = jnp.minimum(pick(first_tile) + (v_eff - pick(vis_start)) + (v - v_eff), n_tiles - 1)
    lo = jnp.maximum(pick(blk_start), tile * TILE_SUBS) - tile * TILE_SUBS
    hi = jnp.minimum(pick(blk_end), (tile + 1) * TILE_SUBS) - tile * TILE_SUBS
    lo = jnp.where(valid, lo, 0)
    hi = jnp.where(valid, hi, 0)
    prev_tile = jnp.concatenate([jnp.full((1,), -1, I32), tile[:-1]])
    first = (tile != prev_tile).astype(I32)
    visit = tuple(a.astype(I32) for a in (tile, e, lo, hi, first))
    return pad_first.astype(I32), n_pad.astype(I32), visit


def _combine_body(dest_sm, rows_ref, x1_ref, gate_ref, mod_ref, fg_ref, o_ref, buf, sem, *, n_tiles):
    tc = x1_ref.shape[0]
    i = pl.program_id(0)

    def row_copy(d, slot, k, t):
        return pltpu.make_async_copy(rows_ref.at[pl.ds(d, 1), :], buf.at[slot, k, pl.ds(t, 1), :], sem.at[slot])

    def issue(tile, slot):
        base = tile * (tc * TOP_K)

        def one(g, carry):
            t0 = pl.multiple_of(g * SUBLANES, SUBLANES)
            first = base + t0 * TOP_K
            for i in range(SUBLANES):
                for k in range(TOP_K):
                    row_copy(dest_sm[first + (i * TOP_K + k)], slot, k, t0 + i).start(priority=k % 2)
            return carry

        lax.fori_loop(0, tc // SUBLANES, one, 0)

    @pl.when(i == 0)
    def _():
        issue(0, 0)

    @pl.when(i + 1 < n_tiles)
    def _():
        issue(i + 1, (i + 1) % 2)

    slot = i % 2

    for k in range(TOP_K):
        pltpu.make_async_copy(rows_ref.at[pl.ds(0, tc), :], buf.at[slot, k], sem.at[slot]).wait()

    g = gate_ref[...]
    y = g[:, 0:1] * buf[slot, 0]
    for k in range(1, TOP_K):
        y = y + g[:, k:k + 1] * buf[slot, k]
    gate_f = mod_ref[:, 5 * D_MODEL:6 * D_MODEL]
    o_ref[...] = _rms(x1_ref[...] + gate_f * y, fg_ref[...])


def _combine(dest_flat, out_rows, x1, gates, mod3, seq, final_g):
    t, d = x1.shape
    tc = COMBINE_ROWS
    tps = seq // tc
    n_tiles = t // tc
    return pl.pallas_call(
        functools.partial(_combine_body, n_tiles=n_tiles),
        grid_spec=pltpu.PrefetchScalarGridSpec(
            num_scalar_prefetch=1,
            grid=(n_tiles,),
            in_specs=[pl.BlockSpec(memory_space=pl.ANY),
                      pl.BlockSpec((tc, d), lambda i, s: (i, 0)),
                      pl.BlockSpec((tc, TOP_K), lambda i, s: (i, 0)),
                      pl.BlockSpec((None, 1, mod3.shape[2]), lambda i, s: (i // tps, 0, 0)),
                      pl.BlockSpec((1, d), lambda i, s: (0, 0))],
            out_specs=pl.BlockSpec((tc, d), lambda i, s: (i, 0)),
            scratch_shapes=[pltpu.VMEM((2, TOP_K, tc, d), F32),
                            pltpu.SemaphoreType.DMA((2,))],
        ),
        out_shape=jax.ShapeDtypeStruct((t, d), F32),
        compiler_params=_params(1),
        name="combine",
    )(dest_flat, out_rows, x1, gates, mod3, final_g)


def kernel(x, c, mix_norm_g, w_ada, b_ada, w_in, gmlp_ln_g, gmlp_ln_b, gmlp_ws, gmlp_bs, pool_w, pool_scale,
           gmlp_out_g, pool_out_g, w_out, ffn_norm_g, router_w, router_b, moe_w1, moe_b1, moe_w2, moe_b2,
           final_norm_g):
    bsz, seq, d = x.shape
    t = bsz * seq
    assert d == D_MODEL and w_ada.shape[0] == 1, "single-layer block with d_model 2048"
    assert seq % MIX_ROWS == 0 and seq % COMBINE_ROWS == 0 and t % ROUTE_ROWS == 0 and t % DISPATCH_ROWS == 0
    row = lambda a: a.reshape(1, -1)

    c_pad = jnp.zeros((8, d), F32).at[:bsz].set(c)
    mod3 = _ada(c_pad, w_ada[0], row(b_ada[0]))[:bsz].reshape(bsz, 1, 6 * d)

    ws2 = jnp.tile(gmlp_ws[0], (1, 2, 2))
    bs2 = jnp.tile(gmlp_bs[0], (1, 2))[:, :, None]
    rw_hi = router_w[0].astype(BF16)
    rw_lo = (router_w[0] - rw_hi.astype(F32)).astype(BF16)
    lane_pad = lambda a: jnp.pad(a, ((0, 0), (0, LANES - a.shape[1])))
    rw = jnp.concatenate([lane_pad(rw_hi), lane_pad(rw_lo)], axis=1)
    rb = lane_pad(row(router_b[0]))

    x1, h2p, logits, w2b = _mix(
        x.reshape(t, d), mod3, seq, moe_w2[0], row(mix_norm_g[0]), w_in[0].astype(BF16), row(gmlp_ln_g[0]),
        row(gmlp_ln_b[0]), ws2, bs2, pool_w[0].astype(BF16), row(pool_scale[0]), row(gmlp_out_g[0]),
        row(pool_out_g[0]), w_out[0].astype(BF16), row(ffn_norm_g[0]), rw, rb)

    dest, gates, counts, w1gb = _route(logits, moe_w1[0])
    dest_flat = dest.reshape(t * TOP_K)

    n_asg = t * TOP_K
    n_rows = -(-(n_asg + N_EXPERTS * SUB_ROWS) // TILE_ROWS) * TILE_ROWS
    pad_first, n_pad, tables = _layout_tables(counts[0, :N_EXPERTS].astype(I32), n_rows // TILE_ROWS)
    rows, w1lb = _dispatch(dest_flat, pad_first, n_pad, h2p, moe_w1[0], n_rows)
    e, f = N_EXPERTS, EXPERT_DIM
    out_rows = _experts(tables, rows, w1gb, w1lb, moe_b1[0].reshape(e, 1, 2 * f), w2b, moe_b2[0].reshape(e, 1, d))

    y = _combine(dest_flat, out_rows, x1, gates, mod3, seq, row(final_norm_g))
    return y.reshape(bsz, seq, d)
```

```python
import functools

import jax
import jax.numpy as jnp
from jax import lax
from jax.experimental import pallas as pl
from jax.experimental.pallas import tpu as pltpu

F32 = jnp.float32
BF16 = jnp.bfloat16
I32 = jnp.int32
U32 = jnp.uint32

D_MODEL = 2048
GMLP_WIDTH = 1024
HEAD_DIM = 128
N_HEADS = GMLP_WIDTH // HEAD_DIM
GMLP_BLOCK = 128
CHUNK = 64
POOL_WIDTH = 1024
POOL_WINDOWS = (2, 4, 8, 16)
POOL_GROUP_DIM = POOL_WIDTH // len(POOL_WINDOWS)
POOL_HALO = 16
N_EXPERTS = 32
TOP_K = 4
EXPERT_DIM = 2048
SWIGLU_ALPHA = 1.702
SWIGLU_LIMIT = 7.0
EPS = 1e-5

LANES = 128
SUBLANES = 8
SUB_ROWS = 256
TILE_SUBS = 4
TILE_ROWS = SUB_ROWS * TILE_SUBS
F_CHUNK = 512
N_FCHUNKS = EXPERT_DIM // F_CHUNK
HALF = D_MODEL // 2

MIX_ROWS = 256
ROUTE_ROWS = 512
DISPATCH_ROWS = 256
COMBINE_ROWS = 256
ADA_COLS = 1024

VMEM_LIMIT = 56 * 1024 * 1024


def _params(n_axes, vmem=VMEM_LIMIT):
    return pltpu.CompilerParams(dimension_semantics=("arbitrary",) * n_axes, vmem_limit_bytes=vmem)


def _rms(x, g):
    return x * lax.rsqrt(jnp.mean(x * x, axis=-1, keepdims=True) + EPS) * g


def _gelu(x):
    return 0.5 * x * (1.0 + lax.erf(x * (2.0 ** -0.5)))


def _ada_body(c_ref, w_ref, b_ref, o_ref):
    c = c_ref[...]
    cond = c * jax.nn.sigmoid(c)
    o_ref[...] = jnp.dot(cond.astype(BF16), w_ref[...].astype(BF16), preferred_element_type=F32) + b_ref[...]


def _ada(c_pad, w_ada, b_ada):
    rows, d = c_pad.shape
    n = w_ada.shape[1]
    return pl.pallas_call(
        _ada_body,
        grid=(n // ADA_COLS,),
        in_specs=[pl.BlockSpec((rows, d), lambda j: (0, 0)),
                  pl.BlockSpec((d, ADA_COLS), lambda j: (0, j)),
                  pl.BlockSpec((1, ADA_COLS), lambda j: (0, j))],
        out_specs=pl.BlockSpec((rows, ADA_COLS), lambda j: (0, j)),
        out_shape=jax.ShapeDtypeStruct((rows, n), F32),
        compiler_params=_params(1),
        name="ada",
    )(c_pad, w_ada, b_ada)


def _pack_bf16_pair(lo, hi):
    lo_bits = pltpu.bitcast(lo.astype(BF16).astype(F32), U32)
    hi_bits = pltpu.bitcast(hi.astype(BF16).astype(F32), U32)
    return hi_bits | (lo_bits >> 16)


def _unpack_bf16_pair(p):
    lo = pltpu.bitcast(p << 16, F32).astype(BF16)
    hi = pltpu.bitcast(p & jnp.uint32(0xFFFF0000), F32).astype(BF16)
    return lo, hi


def _mix_body(x_ref, mod_ref, w2_ref, mixg_ref, win_ref, lng_ref, lnb_ref, ws_ref, bs_ref, pw_ref, pscale_ref,
              gog_ref, pog_ref, wout_ref, ffng_ref, rw_ref, rb_ref,
              x1_ref, h2p_ref, logit_ref, w2b_ref,
              pe_ref, ab_ref, cat_ref, *, tiles_per_seq):
    tr = x_ref.shape[0]
    w2b_ref[...] = w2_ref[...].astype(BF16)
    d = D_MODEL
    seq_tile = pl.program_id(0) % tiles_per_seq
    x = x_ref[...]
    shift_m = mod_ref[:, 0 * d:1 * d]
    scale_m = mod_ref[:, 1 * d:2 * d]
    gate_m = mod_ref[:, 2 * d:3 * d]
    shift_f = mod_ref[:, 3 * d:4 * d]
    scale_f = mod_ref[:, 4 * d:5 * d]

    hb = (_rms(x, mixg_ref[...]) * (1.0 + scale_m) + shift_m).astype(BF16)

    u = _gelu(jnp.dot(hb, win_ref[:, 0:GMLP_WIDTH], preferred_element_type=F32))
    v = _gelu(jnp.dot(hb, win_ref[:, GMLP_WIDTH:2 * GMLP_WIDTH], preferred_element_type=F32))
    slab = 2 * GMLP_BLOCK
    ri = lax.broadcasted_iota(I32, (slab, slab), 0)
    ci = lax.broadcasted_iota(I32, (slab, slab), 1)
    same_block = (ri // GMLP_BLOCK) == (ci // GMLP_BLOCK)
    causal = ((ri % GMLP_BLOCK) // CHUNK) >= ((ci % GMLP_BLOCK) // CHUNK)
    keep = same_block & causal
    for h in range(N_HEADS):
        sl = slice(h * HEAD_DIM, (h + 1) * HEAD_DIM)
        vh = v[:, sl]
        dv = vh - jnp.mean(vh, axis=-1, keepdims=True)
        var = jnp.mean(dv * dv, axis=-1, keepdims=True)
        vn = (dv * lax.rsqrt(var + EPS) * lng_ref[:, sl] + lnb_ref[:, sl]).astype(BF16)
        w_sp = jnp.where(keep, ws_ref[h], 0.0).astype(BF16)
        for s in range(tr // slab):
            rows = slice(s * slab, (s + 1) * slab)
            mixed = jnp.dot(w_sp, vn[rows], preferred_element_type=F32) + bs_ref[h]
            a = u[rows, sl] * mixed
            ab_ref[rows, sl] = a
    a_all = ab_ref[...]
    ssq_a = jnp.sum(a_all * a_all, axis=-1, keepdims=True)
    cat_ref[:, 0:GMLP_WIDTH] = (a_all * lax.rsqrt(ssq_a / GMLP_WIDTH + EPS) * gog_ref[...]).astype(BF16)

    p = jnp.dot(hb, win_ref[:, 2 * GMLP_WIDTH:], preferred_element_type=F32)

    @pl.when(seq_tile == 0)
    def _():
        pe_ref[0:POOL_HALO, :] = jnp.zeros((POOL_HALO, POOL_WIDTH), F32)

    pe_ref[POOL_HALO:, :] = p
    pos1 = (seq_tile * tr + lax.broadcasted_iota(I32, (tr, 1), 0) + 1).astype(F32)
    for g, w in enumerate(POOL_WINDOWS):
        cs = slice(g * POOL_GROUP_DIM, (g + 1) * POOL_GROUP_DIM)
        e = pe_ref[:, cs]
        s = e
        shift = 1
        while shift < w:
            s = s + pltpu.roll(s, shift, 0)
            shift *= 2
        inv = 1.0 / jnp.minimum(pos1, float(w))
        pooled = s[POOL_HALO:] * inv - e[POOL_HALO:]
        y = jnp.dot(pooled.astype(BF16), pw_ref[g], preferred_element_type=F32) * pscale_ref[:, cs]
        ab_ref[:, cs] = y
    pe_ref[0:POOL_HALO, :] = pe_ref[tr:tr + POOL_HALO, :]
    b_all = ab_ref[...]
    ssq_b = jnp.sum(b_all * b_all, axis=-1, keepdims=True)
    cat_ref[:, GMLP_WIDTH:] = (b_all * lax.rsqrt(ssq_b / POOL_WIDTH + EPS) * pog_ref[...]).astype(BF16)

    x1 = x + gate_m * jnp.dot(cat_ref[...], wout_ref[...], preferred_element_type=F32)
    x1_ref[...] = x1
    h2 = _rms(x1, ffng_ref[...]) * (1.0 + scale_f) + shift_f
    h2p_ref[...] = _pack_bf16_pair(h2[:, :HALF], h2[:, HALF:])
    h_hi = h2.astype(BF16)
    h_lo = (h2 - h_hi.astype(F32)).astype(BF16)
    l_hi = jnp.dot(h_hi, rw_ref[...], preferred_element_type=F32)
    l_lo = jnp.dot(h_lo, rw_ref[:, 0:LANES], preferred_element_type=F32)
    logit_ref[...] = l_hi[:, 0:LANES] + l_hi[:, LANES:] + l_lo + rb_ref[...]


def _const_spec(shape):
    nd = len(shape)
    return pl.BlockSpec(shape, lambda i: (0,) * nd, pipeline_mode=pl.Buffered(1))


def _mix(x2d, mod3, seq, w2, mix_g, w_in, ln_g, ln_b, ws2, bs2, pool_w, pool_scale, go_g, po_g, w_out, ffn_g, rw, rb):
    t, d = x2d.shape
    tr = MIX_ROWS
    tps = seq // tr
    n_e, f, _ = w2.shape
    slabs, rem = divmod(t // tr, n_e)
    assert rem == 0 and f % slabs == 0, "mix steps must split the expert weights evenly"
    slab_spec = pl.BlockSpec((None, f // slabs, d), lambda i: (i // slabs, i % slabs, 0))
    row_spec = lambda cols: pl.BlockSpec((tr, cols), lambda i: (i, 0))
    consts = [mix_g, w_in, ln_g, ln_b, ws2, bs2, pool_w, pool_scale, go_g, po_g, w_out, ffn_g, rw, rb]
    return pl.pallas_call(
        functools.partial(_mix_body, tiles_per_seq=tps),
        grid=(t // tr,),
        in_specs=[row_spec(d),
                  pl.BlockSpec((None, 1, mod3.shape[2]), lambda i: (i // tps, 0, 0)),
                  slab_spec]
                 + [_const_spec(a.shape) for a in consts],
        out_specs=[row_spec(d), row_spec(HALF), row_spec(LANES), slab_spec],
        out_shape=[jax.ShapeDtypeStruct((t, d), F32),
                   jax.ShapeDtypeStruct((t, HALF), U32),
                   jax.ShapeDtypeStruct((t, LANES), F32),
                   jax.ShapeDtypeStruct(w2.shape, BF16)],
        scratch_shapes=[pltpu.VMEM((tr + POOL_HALO, POOL_WIDTH), F32),
                        pltpu.VMEM((tr, GMLP_WIDTH), F32),
                        pltpu.VMEM((tr, d), BF16)],
        compiler_params=_params(1),
        name="mix",
    )(x2d, mod3, w2, *consts)


def _route_body(lg_ref, w1g_ref, dest_ref, gate_ref, cnt_ref, w1gb_ref, tot_ref, run_ref, start_ref):
    phase = pl.program_id(0)
    i = pl.program_id(1)
    tt = lg_ref.shape[0]
    w1gb_ref[...] = w1g_ref[...].astype(BF16)
    lane = lax.broadcasted_iota(I32, (tt, LANES), 1)
    l = jnp.where(lane < N_EXPERTS, lg_ref[...], -jnp.inf)
    sels, vals = [], []
    for _ in range(TOP_K):
        m = jnp.max(l, axis=1, keepdims=True)
        idx = jnp.min(jnp.where(l == m, lane, LANES), axis=1, keepdims=True)
        sel = lane == idx
        sels.append(sel)
        vals.append(m)
        l = jnp.where(sel, -jnp.inf, l)
    onehot = sels[0].astype(F32)
    for sel in sels[1:]:
        onehot = onehot + sel.astype(F32)
    colsum = jnp.sum(onehot, axis=0, keepdims=True)

    @pl.when((phase == 0) & (i == 0))
    def _():
        tot_ref[...] = jnp.zeros_like(tot_ref)

    @pl.when(phase == 0)
    def _():
        tot_ref[...] += colsum

    @pl.when((phase == 1) & (i == 0))
    def _():
        tot = tot_ref[...]
        padded = jnp.floor((tot + (SUB_ROWS - 1)) / SUB_ROWS) * SUB_ROWS
        r = lax.broadcasted_iota(I32, (LANES, LANES), 0)
        c = lax.broadcasted_iota(I32, (LANES, LANES), 1)
        col = jnp.sum(jnp.where(r == c, jnp.broadcast_to(padded, (LANES, LANES)), 0.0), axis=1, keepdims=True)
        start_ref[...] = jnp.sum(jnp.where(r < c, col, 0.0), axis=0, keepdims=True)
        run_ref[...] = jnp.zeros_like(run_ref)
        cnt_ref[...] = tot

    @pl.when(phase == 1)
    def _():
        r = lax.broadcasted_iota(I32, (tt, tt), 0)
        c = lax.broadcasted_iota(I32, (tt, tt), 1)
        earlier = (r > c).astype(BF16)
        prefix = jnp.dot(earlier, onehot.astype(BF16), preferred_element_type=F32)
        base = prefix + run_ref[...] + start_ref[...]
        dest = [jnp.sum(jnp.where(sel, base, 0.0), axis=1, keepdims=True) for sel in sels]
        dest_ref[...] = jnp.concatenate(dest, axis=1).astype(I32)
        ex = [jnp.exp(vk - vals[0]) for vk in vals]
        den = ex[0] + ex[1] + ex[2] + ex[3]
        gate_ref[...] = jnp.concatenate([e / den for e in ex], axis=1)
        run_ref[...] += colsum


def _route(logits, w1):
    t = logits.shape[0]
    tt = ROUTE_ROWS
    n_i = t // tt
    n_e, d, f2 = w1.shape
    nj = N_FCHUNKS
    assert 2 * n_i == n_e * nj and f2 == 2 * nj * F_CHUNK, "one up-projection chunk per route step"
    return pl.pallas_call(
        _route_body,
        grid=(2, n_i),
        in_specs=[pl.BlockSpec((tt, LANES), lambda p, i: (i, 0)),
                  pl.BlockSpec((None, d, F_CHUNK), lambda p, i: ((p * n_i + i) // nj, 0, (p * n_i + i) % nj))],
        out_specs=[pl.BlockSpec((tt, TOP_K), lambda p, i: (i * p, 0)),
                   pl.BlockSpec((tt, TOP_K), lambda p, i: (i * p, 0)),
                   pl.BlockSpec((1, LANES), lambda p, i: (0, 0)),
                   pl.BlockSpec((None, None, d, F_CHUNK),
                                lambda p, i: ((p * n_i + i) // nj, (p * n_i + i) % nj, 0, 0))],
        out_shape=[jax.ShapeDtypeStruct((t, TOP_K), I32),
                   jax.ShapeDtypeStruct((t, TOP_K), F32),
                   jax.ShapeDtypeStruct((1, LANES), F32),
                   jax.ShapeDtypeStruct((n_e, nj, d, F_CHUNK), BF16)],
        scratch_shapes=[pltpu.VMEM((1, LANES), F32)] * 3,
        compiler_params=_params(2),
        name="route",
    )(logits, w1)


def _dispatch_body(dest_sm, padfirst_sm, npad_sm, h2p_ref, w1l_ref, rows_ref, w1lb_ref, zero_ref, sem, zsem):
    td = h2p_ref.shape[0]
    step = pl.program_id(0)
    base = step * (td * TOP_K)
    w1lb_ref[...] = w1l_ref[...].astype(BF16)

    def issue(g, carry):
        t0 = pl.multiple_of(g * SUBLANES, SUBLANES)
        first = base + t0 * TOP_K
        for i in range(SUBLANES):
            for k in range(TOP_K):
                d = dest_sm[first + (i * TOP_K + k)]
                pltpu.make_async_copy(h2p_ref.at[pl.ds(t0 + i, 1), :], rows_ref.at[pl.ds(d, 1), :],
                                      sem).start(priority=k % 2)
        return carry

    lax.fori_loop(0, td // SUBLANES, issue, 0)

    @pl.when(step == 0)
    def _():
        zero_ref[...] = jnp.zeros_like(zero_ref)

        def zero_copy(r):
            return pltpu.make_async_copy(zero_ref, rows_ref.at[pl.ds(r, 1), :], zsem)

        def per_range(e, carry):
            first = padfirst_sm[e]
            n_pad = npad_sm[e]

            def start(r, c):
                zero_copy(first + r).start()
                return c

            def wait(r, c):
                zero_copy(first + r).wait()
                return c

            lax.fori_loop(0, n_pad, start, 0)
            lax.fori_loop(0, n_pad, wait, 0)
            return carry

        lax.fori_loop(0, N_EXPERTS + 1, per_range, 0)

    for k in range(TOP_K):
        pltpu.make_async_copy(h2p_ref, rows_ref.at[pl.ds(0, td), :], sem).wait()


def _dispatch(dest_flat, pad_first, n_pad, h2p, w1, n_rows):
    t = h2p.shape[0]
    td = DISPATCH_ROWS
    n_e, d, f2 = w1.shape
    nj = N_FCHUNKS
    assert t // td == n_e * nj and f2 == 2 * nj * F_CHUNK, "one up-projection chunk per dispatch step"
    return pl.pallas_call(
        _dispatch_body,
        grid_spec=pltpu.PrefetchScalarGridSpec(
            num_scalar_prefetch=3,
            grid=(t // td,),
            in_specs=[pl.BlockSpec((td, HALF), lambda i, *_: (i, 0)),
                      pl.BlockSpec((None, d, F_CHUNK), lambda i, *_: (i // nj, 0, nj + i % nj))],
            out_specs=[pl.BlockSpec(memory_space=pl.ANY),
                       pl.BlockSpec((None, None, d, F_CHUNK), lambda i, *_: (i // nj, i % nj, 0, 0))],
            scratch_shapes=[pltpu.VMEM((1, HALF), U32), pltpu.SemaphoreType.DMA, pltpu.SemaphoreType.DMA],
        ),
        out_shape=[jax.ShapeDtypeStruct((n_rows, HALF), U32),
                   jax.ShapeDtypeStruct((n_e, nj, d, F_CHUNK), BF16)],
        compiler_params=_params(1),
        name="dispatch",
    )(dest_flat, pad_first, n_pad, h2p, w1)


def _expert_body(vt, ve, vlo, vhi, vzero, x_ref, w1g_ref, w1l_ref, b1g_ref, b1l_ref, w2_ref, b2_ref, o_ref):
    del vt, ve
    v = pl.program_id(0)
    j = pl.program_id(1)
    lo = vlo[v]
    hi = vhi[v]

    @pl.when((j == 0) & (vzero[v] == 1))
    def _():
        o_ref[...] = jnp.zeros_like(o_ref)

    def visit(first_chunk):
        def run(first_sub, n_sub):
            rows = pl.ds(pl.multiple_of(first_sub * SUB_ROWS, SUB_ROWS), n_sub * SUB_ROWS)
            x_lo, x_hi = _unpack_bf16_pair(x_ref[rows, :])
            a_g = (jnp.dot(x_lo, w1g_ref[0:HALF, :], preferred_element_type=F32)
                   + jnp.dot(x_hi, w1g_ref[HALF:, :], preferred_element_type=F32) + b1g_ref[...])
            a_l = (jnp.dot(x_lo, w1l_ref[0:HALF, :], preferred_element_type=F32)
                   + jnp.dot(x_hi, w1l_ref[HALF:, :], preferred_element_type=F32) + b1l_ref[...])
            glu = jnp.minimum(a_g, SWIGLU_LIMIT)
            lin = jnp.clip(a_l, -SWIGLU_LIMIT, SWIGLU_LIMIT)
            act = glu * jax.nn.sigmoid(SWIGLU_ALPHA * glu) * (lin + 1.0)
            part = jnp.dot(act.astype(BF16), w2_ref[...], preferred_element_type=F32)
            if first_chunk:
                o_ref[rows, :] = part + b2_ref[...]
            else:
                o_ref[rows, :] += part

        n_sub = hi - lo

        @pl.when(n_sub == TILE_SUBS)
        def _():
            run(0, TILE_SUBS // 2)
            run(TILE_SUBS // 2, TILE_SUBS // 2)

        @pl.when((n_sub > 0) & (n_sub < TILE_SUBS))
        def _():
            def pair(p, carry):
                run(lo + 2 * p, 2)
                return carry

            lax.fori_loop(0, lax.shift_right_logical(n_sub, 1), pair, 0)

            @pl.when((n_sub & 1) == 1)
            def _():
                run(hi - 1, 1)

    pl.when(j == 0)(functools.partial(visit, True))
    pl.when(j > 0)(functools.partial(visit, False))


def _experts(tables, rows, w1gb, w1lb, b1, w2b, b2):
    n_rows = rows.shape[0]
    n_visits = tables[0].shape[0]
    nj = N_FCHUNKS

    def jeff(j, vlo, vhi, v):
        return jnp.where(vhi[v] > vlo[v], j, nj - 1)

    return pl.pallas_call(
        _expert_body,
        grid_spec=pltpu.PrefetchScalarGridSpec(
            num_scalar_prefetch=5,
            grid=(n_visits, nj),
            in_specs=[
                pl.BlockSpec((TILE_ROWS, HALF), lambda v, j, vt, ve, vlo, vhi, vf: (vt[v], 0)),
                pl.BlockSpec((None, None, D_MODEL, F_CHUNK),
                             lambda v, j, vt, ve, vlo, vhi, vf: (ve[v], jeff(j, vlo, vhi, v), 0, 0)),
                pl.BlockSpec((None, None, D_MODEL, F_CHUNK),
                             lambda v, j, vt, ve, vlo, vhi, vf: (ve[v], jeff(j, vlo, vhi, v), 0, 0)),
                pl.BlockSpec((None, 1, F_CHUNK),
                             lambda v, j, vt, ve, vlo, vhi, vf: (ve[v], 0, jeff(j, vlo, vhi, v))),
                pl.BlockSpec((None, 1, F_CHUNK),
                             lambda v, j, vt, ve, vlo, vhi, vf: (ve[v], 0, nj + jeff(j, vlo, vhi, v))),
                pl.BlockSpec((None, F_CHUNK, D_MODEL),
                             lambda v, j, vt, ve, vlo, vhi, vf: (ve[v], jeff(j, vlo, vhi, v), 0)),
                pl.BlockSpec((None, 1, D_MODEL), lambda v, j, vt, ve, vlo, vhi, vf: (ve[v], 0, 0)),
            ],
            out_specs=pl.BlockSpec((TILE_ROWS, D_MODEL), lambda v, j, vt, ve, vlo, vhi, vf: (vt[v], 0)),
        ),
        out_shape=jax.ShapeDtypeStruct((n_rows, D_MODEL), F32),
        compiler_params=_params(2),
        name="experts",
    )(*tables, rows, w1gb, w1lb, b1, b1, w2b, b2)


def _layout_tables(counts, n_tiles):
    nblk = (counts + (SUB_ROWS - 1)) // SUB_ROWS
    blk_end = jnp.cumsum(nblk)
    blk_start = blk_end - nblk
    used_rows = blk_end[-1:] * SUB_ROWS
    pad_first = jnp.concatenate([blk_start * SUB_ROWS + counts, used_rows])
    n_pad = jnp.concatenate([nblk * SUB_ROWS - counts, n_tiles * TILE_ROWS - used_rows])

    n_visits = n_tiles + N_EXPERTS
    first_tile = blk_start // TILE_SUBS
    last_tile = (blk_end - 1) // TILE_SUBS
    nvis = jnp.where(nblk > 0, last_tile - first_tile + 1, 0)
    vis_end = jnp.cumsum(nvis)
    vis_start = vis_end - nvis
    total = vis_end[-1]
    v = jnp.arange(n_visits, dtype=I32)
    valid = v < total
    v_eff = jnp.minimum(v, total - 1)
    e = jnp.minimum(jnp.sum((vis_end[None, :] <= v_eff[:, None]).astype(I32), axis=1), N_EXPERTS - 1)
    onehot = (e[:, None] == jnp.arange(N_EXPERTS, dtype=I32)[None, :]).astype(I32)
    pick = lambda a: jnp.sum(onehot * a[None, :], axis=1)
    tile = jnp.minimum(pick(first_tile) + (v_eff - pick(vis_start)) + (v - v_eff), n_tiles - 1)
    lo = jnp.maximum(pick(blk_start), tile * TILE_SUBS) - tile * TILE_SUBS
    hi = jnp.minimum(pick(blk_end), (tile + 1) * TILE_SUBS) - tile * TILE_SUBS
    lo = jnp.where(valid, lo, 0)
    hi = jnp.where(valid, hi, 0)
    prev_tile = jnp.concatenate([jnp.full((1,), -1, I32), tile[:-1]])
    zero = ((tile != prev_tile) & ((tile + 1) * TILE_SUBS > blk_end[-1])).astype(I32)
    visit = tuple(a.astype(I32) for a in (tile, e, lo, hi, zero))
    return pad_first.astype(I32), n_pad.astype(I32), visit


def _combine_body(dest_sm, rows_ref, x1_ref, gate_ref, mod_ref, fg_ref, o_ref, buf, sem, *, n_tiles):
    tc = x1_ref.shape[0]
    i = pl.program_id(0)

    def row_copy(d, slot, k, t):
        return pltpu.make_async_copy(rows_ref.at[pl.ds(d, 1), :], buf.at[slot, k, pl.ds(t, 1), :], sem.at[slot])

    def issue(tile, slot):
        base = tile * (tc * TOP_K)

        def one(g, carry):
            t0 = pl.multiple_of(g * SUBLANES, SUBLANES)
            first = base + t0 * TOP_K
            for i in range(SUBLANES):
                for k in range(TOP_K):
                    row_copy(dest_sm[first + (i * TOP_K + k)], slot, k, t0 + i).start(priority=k % 2)
            return carry

        lax.fori_loop(0, tc // SUBLANES, one, 0)

    @pl.when(i == 0)
    def _():
        issue(0, 0)

    @pl.when(i + 1 < n_tiles)
    def _():
        issue(i + 1, (i + 1) % 2)

    slot = i % 2

    for k in range(TOP_K):
        pltpu.make_async_copy(rows_ref.at[pl.ds(0, tc), :], buf.at[slot, k], sem.at[slot]).wait()

    g = gate_ref[...]
    y = g[:, 0:1] * buf[slot, 0]
    for k in range(1, TOP_K):
        y = y + g[:, k:k + 1] * buf[slot, k]
    gate_f = mod_ref[:, 5 * D_MODEL:6 * D_MODEL]
    o_ref[...] = _rms(x1_ref[...] + gate_f * y, fg_ref[...])


def _combine(dest_flat, out_rows, x1, gates, mod3, seq, final_g):
    t, d = x1.shape
    tc = COMBINE_ROWS
    tps = seq // tc
    n_tiles = t // tc
    return pl.pallas_call(
        functools.partial(_combine_body, n_tiles=n_tiles),
        grid_spec=pltpu.PrefetchScalarGridSpec(
            num_scalar_prefetch=1,
            grid=(n_tiles,),
            in_specs=[pl.BlockSpec(memory_space=pl.ANY),
                      pl.BlockSpec((tc, d), lambda i, s: (i, 0)),
                      pl.BlockSpec((tc, TOP_K), lambda i, s: (i, 0)),
                      pl.BlockSpec((None, 1, mod3.shape[2]), lambda i, s: (i // tps, 0, 0)),
                      pl.BlockSpec((1, d), lambda i, s: (0, 0))],
            out_specs=pl.BlockSpec((tc, d), lambda i, s: (i, 0)),
            scratch_shapes=[pltpu.VMEM((2, TOP_K, tc, d), F32),
                            pltpu.SemaphoreType.DMA((2,))],
        ),
        out_shape=jax.ShapeDtypeStruct((t, d), F32),
        compiler_params=_params(1),
        name="combine",
    )(dest_flat, out_rows, x1, gates, mod3, final_g)


def kernel(x, c, mix_norm_g, w_ada, b_ada, w_in, gmlp_ln_g, gmlp_ln_b, gmlp_ws, gmlp_bs, pool_w, pool_scale,
           gmlp_out_g, pool_out_g, w_out, ffn_norm_g, router_w, router_b, moe_w1, moe_b1, moe_w2, moe_b2,
           final_norm_g):
    bsz, seq, d = x.shape
    t = bsz * seq
    assert d == D_MODEL and w_ada.shape[0] == 1, "single-layer block with d_model 2048"
    assert seq % MIX_ROWS == 0 and seq % COMBINE_ROWS == 0 and t % ROUTE_ROWS == 0 and t % DISPATCH_ROWS == 0
    row = lambda a: a.reshape(1, -1)

    c_pad = jnp.zeros((8, d), F32).at[:bsz].set(c)
    mod3 = _ada(c_pad, w_ada[0], row(b_ada[0]))[:bsz].reshape(bsz, 1, 6 * d)

    ws2 = jnp.tile(gmlp_ws[0], (1, 2, 2))
    bs2 = jnp.tile(gmlp_bs[0], (1, 2))[:, :, None]
    rw_hi = router_w[0].astype(BF16)
    rw_lo = (router_w[0] - rw_hi.astype(F32)).astype(BF16)
    lane_pad = lambda a: jnp.pad(a, ((0, 0), (0, LANES - a.shape[1])))
    rw = jnp.concatenate([lane_pad(rw_hi), lane_pad(rw_lo)], axis=1)
    rb = lane_pad(row(router_b[0]))

    x1, h2p, logits, w2b = _mix(
        x.reshape(t, d), mod3, seq, moe_w2[0], row(mix_norm_g[0]), w_in[0].astype(BF16), row(gmlp_ln_g[0]),
        row(gmlp_ln_b[0]), ws2, bs2, pool_w[0].astype(BF16), row(pool_scale[0]), row(gmlp_out_g[0]),
        row(pool_out_g[0]), w_out[0].astype(BF16), row(ffn_norm_g[0]), rw, rb)

    dest, gates, counts, w1gb = _route(logits, moe_w1[0])
    dest_flat = dest.reshape(t * TOP_K)

    n_asg = t * TOP_K
    n_rows = -(-(n_asg + N_EXPERTS * SUB_ROWS) // TILE_ROWS) * TILE_ROWS
    pad_first, n_pad, tables = _layout_tables(counts[0, :N_EXPERTS].astype(I32), n_rows // TILE_ROWS)
    rows, w1lb = _dispatch(dest_flat, pad_first, n_pad, h2p, moe_w1[0], n_rows)
    e, f = N_EXPERTS, EXPERT_DIM
    out_rows = _experts(tables, rows, w1gb, w1lb, moe_b1[0].reshape(e, 1, 2 * f), w2b, moe_b2[0].reshape(e, 1, d))

    y = _combine(dest_flat, out_rows, x1, gates, mod3, seq, row(final_norm_g))
    return y.reshape(bsz, seq, d)
```

```python
import functools

import jax
import jax.numpy as jnp
from jax import lax
from jax.experimental import pallas as pl
from jax.experimental.pallas import tpu as pltpu

F32 = jnp.float32
BF16 = jnp.bfloat16
I32 = jnp.int32
U32 = jnp.uint32

D_MODEL = 2048
GMLP_WIDTH = 1024
HEAD_DIM = 128
N_HEADS = GMLP_WIDTH // HEAD_DIM
GMLP_BLOCK = 128
CHUNK = 64
POOL_WIDTH = 1024
POOL_WINDOWS = (2, 4, 8, 16)
POOL_GROUP_DIM = POOL_WIDTH // len(POOL_WINDOWS)
POOL_HALO = 16
N_EXPERTS = 32
TOP_K = 4
EXPERT_DIM = 2048
SWIGLU_ALPHA = 1.702
SWIGLU_LIMIT = 7.0
EPS = 1e-5

LANES = 128
SUBLANES = 8
SUB_ROWS = 256
TILE_SUBS = 4
TILE_ROWS = SUB_ROWS * TILE_SUBS
F_CHUNK = 512
N_FCHUNKS = EXPERT_DIM // F_CHUNK
STEP_CHUNKS = 2
N_FSTEPS = N_FCHUNKS // STEP_CHUNKS
HALF = D_MODEL // 2

MIX_ROWS = 256
ROUTE_ROWS = 512
DISPATCH_ROWS = 256
COMBINE_ROWS = 256
ADA_COLS = 1024

VMEM_LIMIT = 56 * 1024 * 1024


def _params(n_axes, vmem=VMEM_LIMIT):
    return pltpu.CompilerParams(dimension_semantics=("arbitrary",) * n_axes, vmem_limit_bytes=vmem)


def _rms(x, g):
    return x * lax.rsqrt(jnp.mean(x * x, axis=-1, keepdims=True) + EPS) * g


def _gelu(x):
    return 0.5 * x * (1.0 + lax.erf(x * (2.0 ** -0.5)))


def _ada_body(c_ref, w_ref, b_ref, o_ref):
    c = c_ref[...]
    cond = c * jax.nn.sigmoid(c)
    o_ref[...] = jnp.dot(cond.astype(BF16), w_ref[...].astype(BF16), preferred_element_type=F32) + b_ref[...]


def _ada(c_pad, w_ada, b_ada):
    rows, d = c_pad.shape
    n = w_ada.shape[1]
    return pl.pallas_call(
        _ada_body,
        grid=(n // ADA_COLS,),
        in_specs=[pl.BlockSpec((rows, d), lambda j: (0, 0)),
                  pl.BlockSpec((d, ADA_COLS), lambda j: (0, j)),
                  pl.BlockSpec((1, ADA_COLS), lambda j: (0, j))],
        out_specs=pl.BlockSpec((rows, ADA_COLS), lambda j: (0, j)),
        out_shape=jax.ShapeDtypeStruct((rows, n), F32),
        compiler_params=_params(1),
        name="ada",
    )(c_pad, w_ada, b_ada)


def _pack_bf16_pair(lo, hi):
    lo_bits = pltpu.bitcast(lo.astype(BF16).astype(F32), U32)
    hi_bits = pltpu.bitcast(hi.astype(BF16).astype(F32), U32)
    return hi_bits | (lo_bits >> 16)


def _unpack_bf16_pair(p):
    lo = pltpu.bitcast(p << 16, F32).astype(BF16)
    hi = pltpu.bitcast(p & jnp.uint32(0xFFFF0000), F32).astype(BF16)
    return lo, hi


def _mix_body(x_ref, mod_ref, w2_ref, mixg_ref, win_ref, lng_ref, lnb_ref, ws_ref, bs_ref, pw_ref, pscale_ref,
              gog_ref, pog_ref, wout_ref, ffng_ref, rw_ref, rb_ref,
              x1_ref, h2p_ref, logit_ref, w2b_ref,
              pe_ref, ab_ref, cat_ref, *, tiles_per_seq):
    tr = x_ref.shape[0]
    w2b_ref[...] = w2_ref[...].astype(BF16)
    d = D_MODEL
    seq_tile = pl.program_id(0) % tiles_per_seq
    x = x_ref[...]
    shift_m = mod_ref[:, 0 * d:1 * d]
    scale_m = mod_ref[:, 1 * d:2 * d]
    gate_m = mod_ref[:, 2 * d:3 * d]
    shift_f = mod_ref[:, 3 * d:4 * d]
    scale_f = mod_ref[:, 4 * d:5 * d]

    hb = (_rms(x, mixg_ref[...]) * (1.0 + scale_m) + shift_m).astype(BF16)

    u = _gelu(jnp.dot(hb, win_ref[:, 0:GMLP_WIDTH], preferred_element_type=F32))
    v = _gelu(jnp.dot(hb, win_ref[:, GMLP_WIDTH:2 * GMLP_WIDTH], preferred_element_type=F32))
    slab = 2 * GMLP_BLOCK
    ri = lax.broadcasted_iota(I32, (slab, slab), 0)
    ci = lax.broadcasted_iota(I32, (slab, slab), 1)
    same_block = (ri // GMLP_BLOCK) == (ci // GMLP_BLOCK)
    causal = ((ri % GMLP_BLOCK) // CHUNK) >= ((ci % GMLP_BLOCK) // CHUNK)
    keep = same_block & causal
    for h in range(N_HEADS):
        sl = slice(h * HEAD_DIM, (h + 1) * HEAD_DIM)
        vh = v[:, sl]
        dv = vh - jnp.mean(vh, axis=-1, keepdims=True)
        var = jnp.mean(dv * dv, axis=-1, keepdims=True)
        vn = (dv * lax.rsqrt(var + EPS) * lng_ref[:, sl] + lnb_ref[:, sl]).astype(BF16)
        w_sp = jnp.where(keep, ws_ref[h], 0.0).astype(BF16)
        for s in range(tr // slab):
            rows = slice(s * slab, (s + 1) * slab)
            mixed = jnp.dot(w_sp, vn[rows], preferred_element_type=F32) + bs_ref[h]
            a = u[rows, sl] * mixed
            ab_ref[rows, sl] = a
    a_all = ab_ref[...]
    ssq_a = jnp.sum(a_all * a_all, axis=-1, keepdims=True)
    cat_ref[:, 0:GMLP_WIDTH] = (a_all * lax.rsqrt(ssq_a / GMLP_WIDTH + EPS) * gog_ref[...]).astype(BF16)

    p = jnp.dot(hb, win_ref[:, 2 * GMLP_WIDTH:], preferred_element_type=F32)

    @pl.when(seq_tile == 0)
    def _():
        pe_ref[0:POOL_HALO, :] = jnp.zeros((POOL_HALO, POOL_WIDTH), F32)

    pe_ref[POOL_HALO:, :] = p
    pos1 = (seq_tile * tr + lax.broadcasted_iota(I32, (tr, 1), 0) + 1).astype(F32)
    for g, w in enumerate(POOL_WINDOWS):
        cs = slice(g * POOL_GROUP_DIM, (g + 1) * POOL_GROUP_DIM)
        e = pe_ref[:, cs]
        s = e
        shift = 1
        while shift < w:
            s = s + pltpu.roll(s, shift, 0)
            shift *= 2
        inv = 1.0 / jnp.minimum(pos1, float(w))
        pooled = s[POOL_HALO:] * inv - e[POOL_HALO:]
        y = jnp.dot(pooled.astype(BF16), pw_ref[g], preferred_element_type=F32) * pscale_ref[:, cs]
        ab_ref[:, cs] = y
    pe_ref[0:POOL_HALO, :] = pe_ref[tr:tr + POOL_HALO, :]
    b_all = ab_ref[...]
    ssq_b = jnp.sum(b_all * b_all, axis=-1, keepdims=True)
    cat_ref[:, GMLP_WIDTH:] = (b_all * lax.rsqrt(ssq_b / POOL_WIDTH + EPS) * pog_ref[...]).astype(BF16)

    x1 = x + gate_m * jnp.dot(cat_ref[...], wout_ref[...], preferred_element_type=F32)
    x1_ref[...] = x1
    h2 = _rms(x1, ffng_ref[...]) * (1.0 + scale_f) + shift_f
    h2p_ref[...] = _pack_bf16_pair(h2[:, :HALF], h2[:, HALF:])
    h_hi = h2.astype(BF16)
    h_lo = (h2 - h_hi.astype(F32)).astype(BF16)
    l_hi = jnp.dot(h_hi, rw_ref[...], preferred_element_type=F32)
    l_lo = jnp.dot(h_lo, rw_ref[:, 0:LANES], preferred_element_type=F32)
    logit_ref[...] = l_hi[:, 0:LANES] + l_hi[:, LANES:] + l_lo + rb_ref[...]


def _const_spec(shape):
    nd = len(shape)
    return pl.BlockSpec(shape, lambda i: (0,) * nd, pipeline_mode=pl.Buffered(1))


def _mix(x2d, mod3, seq, w2, mix_g, w_in, ln_g, ln_b, ws2, bs2, pool_w, pool_scale, go_g, po_g, w_out, ffn_g, rw, rb):
    t, d = x2d.shape
    tr = MIX_ROWS
    tps = seq // tr
    n_e, f, _ = w2.shape
    slabs, rem = divmod(t // tr, n_e)
    assert rem == 0 and f % slabs == 0, "mix steps must split the expert weights evenly"
    slab_spec = pl.BlockSpec((None, f // slabs, d), lambda i: (i // slabs, i % slabs, 0))
    row_spec = lambda cols: pl.BlockSpec((tr, cols), lambda i: (i, 0))
    consts = [mix_g, w_in, ln_g, ln_b, ws2, bs2, pool_w, pool_scale, go_g, po_g, w_out, ffn_g, rw, rb]
    return pl.pallas_call(
        functools.partial(_mix_body, tiles_per_seq=tps),
        grid=(t // tr,),
        in_specs=[row_spec(d),
                  pl.BlockSpec((None, 1, mod3.shape[2]), lambda i: (i // tps, 0, 0)),
                  slab_spec]
                 + [_const_spec(a.shape) for a in consts],
        out_specs=[row_spec(d), row_spec(HALF), row_spec(LANES), slab_spec],
        out_shape=[jax.ShapeDtypeStruct((t, d), F32),
                   jax.ShapeDtypeStruct((t, HALF), U32),
                   jax.ShapeDtypeStruct((t, LANES), F32),
                   jax.ShapeDtypeStruct(w2.shape, BF16)],
        scratch_shapes=[pltpu.VMEM((tr + POOL_HALO, POOL_WIDTH), F32),
                        pltpu.VMEM((tr, GMLP_WIDTH), F32),
                        pltpu.VMEM((tr, d), BF16)],
        compiler_params=_params(1),
        name="mix",
    )(x2d, mod3, w2, *consts)


def _route_body(lg_ref, w1g_ref, dest_ref, gate_ref, cnt_ref, w1gb_ref, tot_ref, run_ref, start_ref):
    phase = pl.program_id(0)
    i = pl.program_id(1)
    tt = lg_ref.shape[0]
    w1gb_ref[...] = w1g_ref[...].astype(BF16)
    lane = lax.broadcasted_iota(I32, (tt, LANES), 1)
    l = jnp.where(lane < N_EXPERTS, lg_ref[...], -jnp.inf)
    sels, vals = [], []
    for _ in range(TOP_K):
        m = jnp.max(l, axis=1, keepdims=True)
        idx = jnp.min(jnp.where(l == m, lane, LANES), axis=1, keepdims=True)
        sel = lane == idx
        sels.append(sel)
        vals.append(m)
        l = jnp.where(sel, -jnp.inf, l)
    onehot = sels[0].astype(F32)
    for sel in sels[1:]:
        onehot = onehot + sel.astype(F32)
    colsum = jnp.sum(onehot, axis=0, keepdims=True)

    @pl.when((phase == 0) & (i == 0))
    def _():
        tot_ref[...] = jnp.zeros_like(tot_ref)

    @pl.when(phase == 0)
    def _():
        tot_ref[...] += colsum

    @pl.when((phase == 1) & (i == 0))
    def _():
        tot = tot_ref[...]
        padded = jnp.floor((tot + (SUB_ROWS - 1)) / SUB_ROWS) * SUB_ROWS
        r = lax.broadcasted_iota(I32, (LANES, LANES), 0)
        c = lax.broadcasted_iota(I32, (LANES, LANES), 1)
        col = jnp.sum(jnp.where(r == c, jnp.broadcast_to(padded, (LANES, LANES)), 0.0), axis=1, keepdims=True)
        start_ref[...] = jnp.sum(jnp.where(r < c, col, 0.0), axis=0, keepdims=True)
        run_ref[...] = jnp.zeros_like(run_ref)
        cnt_ref[...] = tot

    @pl.when(phase == 1)
    def _():
        r = lax.broadcasted_iota(I32, (tt, tt), 0)
        c = lax.broadcasted_iota(I32, (tt, tt), 1)
        earlier = (r > c).astype(BF16)
        prefix = jnp.dot(earlier, onehot.astype(BF16), preferred_element_type=F32)
        base = prefix + run_ref[...] + start_ref[...]
        dest = [jnp.sum(jnp.where(sel, base, 0.0), axis=1, keepdims=True) for sel in sels]
        dest_ref[...] = jnp.concatenate(dest, axis=1).astype(I32)
        ex = [jnp.exp(vk - vals[0]) for vk in vals]
        den = ex[0] + ex[1] + ex[2] + ex[3]
        gate_ref[...] = jnp.concatenate([e / den for e in ex], axis=1)
        run_ref[...] += colsum


def _route(logits, w1):
    t = logits.shape[0]
    tt = ROUTE_ROWS
    n_i = t // tt
    n_e, d, f2 = w1.shape
    nj = N_FCHUNKS
    assert 2 * n_i == n_e * nj and f2 == 2 * nj * F_CHUNK, "one up-projection chunk per route step"
    return pl.pallas_call(
        _route_body,
        grid=(2, n_i),
        in_specs=[pl.BlockSpec((tt, LANES), lambda p, i: (i, 0)),
                  pl.BlockSpec((None, d, F_CHUNK), lambda p, i: ((p * n_i + i) // nj, 0, (p * n_i + i) % nj))],
        out_specs=[pl.BlockSpec((tt, TOP_K), lambda p, i: (i * p, 0)),
                   pl.BlockSpec((tt, TOP_K), lambda p, i: (i * p, 0)),
                   pl.BlockSpec((1, LANES), lambda p, i: (0, 0)),
                   pl.BlockSpec((None, None, d, F_CHUNK),
                                lambda p, i: ((p * n_i + i) // nj, (p * n_i + i) % nj, 0, 0))],
        out_shape=[jax.ShapeDtypeStruct((t, TOP_K), I32),
                   jax.ShapeDtypeStruct((t, TOP_K), F32),
                   jax.ShapeDtypeStruct((1, LANES), F32),
                   jax.ShapeDtypeStruct((n_e, nj, d, F_CHUNK), BF16)],
        scratch_shapes=[pltpu.VMEM((1, LANES), F32)] * 3,
        compiler_params=_params(2),
        name="route",
    )(logits, w1)


def _dispatch_body(dest_sm, padfirst_sm, npad_sm, h2p_ref, w1l_ref, rows_ref, w1lb_ref, zero_ref, sem, zsem):
    td = h2p_ref.shape[0]
    step = pl.program_id(0)
    base = step * (td * TOP_K)
    w1lb_ref[...] = w1l_ref[...].astype(BF16)

    def issue(g, carry):
        t0 = pl.multiple_of(g * SUBLANES, SUBLANES)
        first = base + t0 * TOP_K
        for i in range(SUBLANES):
            for k in range(TOP_K):
                d = dest_sm[first + (i * TOP_K + k)]
                pltpu.make_async_copy(h2p_ref.at[pl.ds(t0 + i, 1), :], rows_ref.at[pl.ds(d, 1), :],
                                      sem).start(priority=k % 2)
        return carry

    lax.fori_loop(0, td // SUBLANES, issue, 0)

    @pl.when(step == 0)
    def _():
        zero_ref[...] = jnp.zeros_like(zero_ref)

        def zero_copy(r):
            return pltpu.make_async_copy(zero_ref, rows_ref.at[pl.ds(r, 1), :], zsem)

        def per_range(e, carry):
            first = padfirst_sm[e]
            n_pad = npad_sm[e]

            def start(r, c):
                zero_copy(first + r).start()
                return c

            def wait(r, c):
                zero_copy(first + r).wait()
                return c

            lax.fori_loop(0, n_pad, start, 0)
            lax.fori_loop(0, n_pad, wait, 0)
            return carry

        lax.fori_loop(0, N_EXPERTS + 1, per_range, 0)

    for k in range(TOP_K):
        pltpu.make_async_copy(h2p_ref, rows_ref.at[pl.ds(0, td), :], sem).wait()


def _dispatch(dest_flat, pad_first, n_pad, h2p, w1, n_rows):
    t = h2p.shape[0]
    td = DISPATCH_ROWS
    n_e, d, f2 = w1.shape
    nj = N_FCHUNKS
    assert t // td == n_e * nj and f2 == 2 * nj * F_CHUNK, "one up-projection chunk per dispatch step"
    return pl.pallas_call(
        _dispatch_body,
        grid_spec=pltpu.PrefetchScalarGridSpec(
            num_scalar_prefetch=3,
            grid=(t // td,),
            in_specs=[pl.BlockSpec((td, HALF), lambda i, *_: (i, 0)),
                      pl.BlockSpec((None, d, F_CHUNK), lambda i, *_: (i // nj, 0, nj + i % nj))],
            out_specs=[pl.BlockSpec(memory_space=pl.ANY),
                       pl.BlockSpec((None, None, d, F_CHUNK), lambda i, *_: (i // nj, i % nj, 0, 0))],
            scratch_shapes=[pltpu.VMEM((1, HALF), U32), pltpu.SemaphoreType.DMA, pltpu.SemaphoreType.DMA],
        ),
        out_shape=[jax.ShapeDtypeStruct((n_rows, HALF), U32),
                   jax.ShapeDtypeStruct((n_e, nj, d, F_CHUNK), BF16)],
        compiler_params=_params(1),
        name="dispatch",
    )(dest_flat, pad_first, n_pad, h2p, w1)


def _expert_body(vt, ve, vlo, vhi, vzero, x_ref, w1g_ref, w1l_ref, b1g_ref, b1l_ref, w2_ref, b2_ref, o_ref):
    del vt, ve
    v = pl.program_id(0)
    j = pl.program_id(1)
    lo = vlo[v]
    hi = vhi[v]

    @pl.when((j == 0) & (vzero[v] == 1))
    def _():
        o_ref[...] = jnp.zeros_like(o_ref)

    def visit(first_chunk):
        def run(first_sub, n_sub):
            rows = pl.ds(pl.multiple_of(first_sub * SUB_ROWS, SUB_ROWS), n_sub * SUB_ROWS)
            x_lo, x_hi = _unpack_bf16_pair(x_ref[rows, :])
            for c in range(STEP_CHUNKS):
                cols = slice(c * F_CHUNK, (c + 1) * F_CHUNK)
                a_g = (jnp.dot(x_lo, w1g_ref[c, 0:HALF, :], preferred_element_type=F32)
                       + jnp.dot(x_hi, w1g_ref[c, HALF:, :], preferred_element_type=F32) + b1g_ref[:, cols])
                a_l = (jnp.dot(x_lo, w1l_ref[c, 0:HALF, :], preferred_element_type=F32)
                       + jnp.dot(x_hi, w1l_ref[c, HALF:, :], preferred_element_type=F32) + b1l_ref[:, cols])
                glu = jnp.minimum(a_g, SWIGLU_LIMIT)
                lin = jnp.clip(a_l, -SWIGLU_LIMIT, SWIGLU_LIMIT)
                act = glu * jax.nn.sigmoid(SWIGLU_ALPHA * glu) * (lin + 1.0)
                part = jnp.dot(act.astype(BF16), w2_ref[cols, :], preferred_element_type=F32)
                if first_chunk and c == 0:
                    o_ref[rows, :] = part + b2_ref[...]
                else:
                    o_ref[rows, :] += part

        n_sub = hi - lo

        @pl.when(n_sub == TILE_SUBS)
        def _():
            run(0, TILE_SUBS // 2)
            run(TILE_SUBS // 2, TILE_SUBS // 2)

        @pl.when((n_sub > 0) & (n_sub < TILE_SUBS))
        def _():
            def pair(p, carry):
                run(lo + 2 * p, 2)
                return carry

            lax.fori_loop(0, lax.shift_right_logical(n_sub, 1), pair, 0)

            @pl.when((n_sub & 1) == 1)
            def _():
                run(hi - 1, 1)

    pl.when(j == 0)(functools.partial(visit, True))
    pl.when(j > 0)(functools.partial(visit, False))


def _experts(tables, rows, w1gb, w1lb, b1, w2b, b2):
    n_rows = rows.shape[0]
    n_visits = tables[0].shape[0]
    nj = N_FSTEPS
    step_cols = STEP_CHUNKS * F_CHUNK

    def jeff(j, vlo, vhi, v):
        return jnp.where(vhi[v] > vlo[v], j, nj - 1)

    return pl.pallas_call(
        _expert_body,
        grid_spec=pltpu.PrefetchScalarGridSpec(
            num_scalar_prefetch=5,
            grid=(n_visits, nj),
            in_specs=[
                pl.BlockSpec((TILE_ROWS, HALF), lambda v, j, vt, ve, vlo, vhi, vf: (vt[v], 0)),
                pl.BlockSpec((None, STEP_CHUNKS, D_MODEL, F_CHUNK),
                             lambda v, j, vt, ve, vlo, vhi, vf: (ve[v], jeff(j, vlo, vhi, v), 0, 0)),
                pl.BlockSpec((None, STEP_CHUNKS, D_MODEL, F_CHUNK),
                             lambda v, j, vt, ve, vlo, vhi, vf: (ve[v], jeff(j, vlo, vhi, v), 0, 0)),
                pl.BlockSpec((None, 1, step_cols),
                             lambda v, j, vt, ve, vlo, vhi, vf: (ve[v], 0, jeff(j, vlo, vhi, v))),
                pl.BlockSpec((None, 1, step_cols),
                             lambda v, j, vt, ve, vlo, vhi, vf: (ve[v], 0, nj + jeff(j, vlo, vhi, v))),
                pl.BlockSpec((None, step_cols, D_MODEL),
                             lambda v, j, vt, ve, vlo, vhi, vf: (ve[v], jeff(j, vlo, vhi, v), 0)),
                pl.BlockSpec((None, 1, D_MODEL), lambda v, j, vt, ve, vlo, vhi, vf: (ve[v], 0, 0)),
            ],
            out_specs=pl.BlockSpec((TILE_ROWS, D_MODEL), lambda v, j, vt, ve, vlo, vhi, vf: (vt[v], 0)),
        ),
        out_shape=jax.ShapeDtypeStruct((n_rows, D_MODEL), F32),
        compiler_params=_params(2),
        name="experts",
    )(*tables, rows, w1gb, w1lb, b1, b1, w2b, b2)


def _layout_tables(counts, n_tiles):
    nblk = (counts + (SUB_ROWS - 1)) // SUB_ROWS
    blk_end = jnp.cumsum(nblk)
    blk_start = blk_end - nblk
    used_rows = blk_end[-1:] * SUB_ROWS
    pad_first = jnp.concatenate([blk_start * SUB_ROWS + counts, used_rows])
    n_pad = jnp.concatenate([nblk * SUB_ROWS - counts, n_tiles * TILE_ROWS - used_rows])

    n_visits = n_tiles + N_EXPERTS
    first_tile = blk_start // TILE_SUBS
    last_tile = (blk_end - 1) // TILE_SUBS
    nvis = jnp.where(nblk > 0, last_tile - first_tile + 1, 0)
    vis_end = jnp.cumsum(nvis)
    vis_start = vis_end - nvis
    total = vis_end[-1]
    v = jnp.arange(n_visits, dtype=I32)
    valid = v < total
    v_eff = jnp.minimum(v, total - 1)
    e = jnp.minimum(jnp.sum((vis_end[None, :] <= v_eff[:, None]).astype(I32), axis=1), N_EXPERTS - 1)
    onehot = (e[:, None] == jnp.arange(N_EXPERTS, dtype=I32)[None, :]).astype(I32)
    pick = lambda a: jnp.sum(onehot * a[None, :], axis=1)
    tile = jnp.minimum(pick(first_tile) + (v_eff - pick(vis_start)) + (v - v_eff), n_tiles - 1)
    lo = jnp.maximum(pick(blk_start), tile * TILE_SUBS) - tile * TILE_SUBS
    hi = jnp.minimum(pick(blk_end), (tile + 1) * TILE_SUBS) - tile * TILE_SUBS
    lo = jnp.where(valid, lo, 0)
    hi = jnp.where(valid, hi, 0)
    prev_tile = jnp.concatenate([jnp.full((1,), -1, I32), tile[:-1]])
    zero = ((tile != prev_tile) & ((tile + 1) * TILE_SUBS > blk_end[-1])).astype(I32)
    visit = tuple(a.astype(I32) for a in (tile, e, lo, hi, zero))
    return pad_first.astype(I32), n_pad.astype(I32), visit


def _combine_body(dest_sm, rows_ref, x1_ref, gate_ref, mod_ref, fg_ref, o_ref, buf, sem, *, n_tiles):
    tc = x1_ref.shape[0]
    i = pl.program_id(0)

    def row_copy(d, slot, k, t):
        return pltpu.make_async_copy(rows_ref.at[pl.ds(d, 1), :], buf.at[slot, k, pl.ds(t, 1), :], sem.at[slot])

    def issue(tile, slot):
        base = tile * (tc * TOP_K)

        def one(g, carry):
            t0 = pl.multiple_of(g * SUBLANES, SUBLANES)
            first = base + t0 * TOP_K
            for i in range(SUBLANES):
                for k in range(TOP_K):
                    row_copy(dest_sm[first + (i * TOP_K + k)], slot, k, t0 + i).start(priority=k % 2)
            return carry

        lax.fori_loop(0, tc // SUBLANES, one, 0)

    @pl.when(i == 0)
    def _():
        issue(0, 0)

    @pl.when(i + 1 < n_tiles)
    def _():
        issue(i + 1, (i + 1) % 2)

    slot = i % 2

    for k in range(TOP_K):
        pltpu.make_async_copy(rows_ref.at[pl.ds(0, tc), :], buf.at[slot, k], sem.at[slot]).wait()

    g = gate_ref[...]
    y = g[:, 0:1] * buf[slot, 0]
    for k in range(1, TOP_K):
        y = y + g[:, k:k + 1] * buf[slot, k]
    gate_f = mod_ref[:, 5 * D_MODEL:6 * D_MODEL]
    o_ref[...] = _rms(x1_ref[...] + gate_f * y, fg_ref[...])


def _combine(dest_flat, out_rows, x1, gates, mod3, seq, final_g):
    t, d = x1.shape
    tc = COMBINE_ROWS
    tps = seq // tc
    n_tiles = t // tc
    return pl.pallas_call(
        functools.partial(_combine_body, n_tiles=n_tiles),
        grid_spec=pltpu.PrefetchScalarGridSpec(
            num_scalar_prefetch=1,
            grid=(n_tiles,),
            in_specs=[pl.BlockSpec(memory_space=pl.ANY),
                      pl.BlockSpec((tc, d), lambda i, s: (i, 0)),
                      pl.BlockSpec((tc, TOP_K), lambda i, s: (i, 0)),
                      pl.BlockSpec((None, 1, mod3.shape[2]), lambda i, s: (i // tps, 0, 0)),
                      pl.BlockSpec((1, d), lambda i, s: (0, 0))],
            out_specs=pl.BlockSpec((tc, d), lambda i, s: (i, 0)),
            scratch_shapes=[pltpu.VMEM((2, TOP_K, tc, d), F32),
                            pltpu.SemaphoreType.DMA((2,))],
        ),
        out_shape=jax.ShapeDtypeStruct((t, d), F32),
        compiler_params=_params(1),
        name="combine",
    )(dest_flat, out_rows, x1, gates, mod3, final_g)


def kernel(x, c, mix_norm_g, w_ada, b_ada, w_in, gmlp_ln_g, gmlp_ln_b, gmlp_ws, gmlp_bs, pool_w, pool_scale,
           gmlp_out_g, pool_out_g, w_out, ffn_norm_g, router_w, router_b, moe_w1, moe_b1, moe_w2, moe_b2,
           final_norm_g):
    bsz, seq, d = x.shape
    t = bsz * seq
    assert d == D_MODEL and w_ada.shape[0] == 1, "single-layer block with d_model 2048"
    assert seq % MIX_ROWS == 0 and seq % COMBINE_ROWS == 0 and t % ROUTE_ROWS == 0 and t % DISPATCH_ROWS == 0
    row = lambda a: a.reshape(1, -1)

    c_pad = jnp.zeros((8, d), F32).at[:bsz].set(c)
    mod3 = _ada(c_pad, w_ada[0], row(b_ada[0]))[:bsz].reshape(bsz, 1, 6 * d)

    ws2 = jnp.tile(gmlp_ws[0], (1, 2, 2))
    bs2 = jnp.tile(gmlp_bs[0], (1, 2))[:, :, None]
    rw_hi = router_w[0].astype(BF16)
    rw_lo = (router_w[0] - rw_hi.astype(F32)).astype(BF16)
    lane_pad = lambda a: jnp.pad(a, ((0, 0), (0, LANES - a.shape[1])))
    rw = jnp.concatenate([lane_pad(rw_hi), lane_pad(rw_lo)], axis=1)
    rb = lane_pad(row(router_b[0]))

    x1, h2p, logits, w2b = _mix(
        x.reshape(t, d), mod3, seq, moe_w2[0], row(mix_norm_g[0]), w_in[0].astype(BF16), row(gmlp_ln_g[0]),
        row(gmlp_ln_b[0]), ws2, bs2, pool_w[0].astype(BF16), row(pool_scale[0]), row(gmlp_out_g[0]),
        row(pool_out_g[0]), w_out[0].astype(BF16), row(ffn_norm_g[0]), rw, rb)

    dest, gates, counts, w1gb = _route(logits, moe_w1[0])
    dest_flat = dest.reshape(t * TOP_K)

    n_asg = t * TOP_K
    n_rows = -(-(n_asg + N_EXPERTS * SUB_ROWS) // TILE_ROWS) * TILE_ROWS
    pad_first, n_pad, tables = _layout_tables(counts[0, :N_EXPERTS].astype(I32), n_rows // TILE_ROWS)
    rows, w1lb = _dispatch(dest_flat, pad_first, n_pad, h2p, moe_w1[0], n_rows)
    e, f = N_EXPERTS, EXPERT_DIM
    out_rows = _experts(tables, rows, w1gb, w1lb, moe_b1[0].reshape(e, 1, 2 * f), w2b, moe_b2[0].reshape(e, 1, d))

    y = _combine(dest_flat, out_rows, x1, gates, mod3, seq, row(final_norm_g))
    return y.reshape(bsz, seq, d)
```

```python
import functools

import jax
import jax.numpy as jnp
from jax import lax
from jax.experimental import pallas as pl
from jax.experimental.pallas import tpu as pltpu

F32 = jnp.float32
BF16 = jnp.bfloat16
I32 = jnp.int32
U32 = jnp.uint32

D_MODEL = 2048
GMLP_WIDTH = 1024
HEAD_DIM = 128
N_HEADS = GMLP_WIDTH // HEAD_DIM
GMLP_BLOCK = 128
CHUNK = 64
POOL_WIDTH = 1024
POOL_WINDOWS = (2, 4, 8, 16)
POOL_GROUP_DIM = POOL_WIDTH // len(POOL_WINDOWS)
POOL_HALO = 16
N_EXPERTS = 32
TOP_K = 4
EXPERT_DIM = 2048
SWIGLU_ALPHA = 1.702
SWIGLU_LIMIT = 7.0
EPS = 1e-5

LANES = 128
SUBLANES = 8
SUB_ROWS = 256
TILE_SUBS = 4
TILE_ROWS = SUB_ROWS * TILE_SUBS
F_CHUNK = 512
N_FCHUNKS = EXPERT_DIM // F_CHUNK
STEP_CHUNKS = 2
N_FSTEPS = N_FCHUNKS // STEP_CHUNKS
HALF = D_MODEL // 2

MIX_ROWS = 256
ROUTE_ROWS = 512
DISPATCH_ROWS = 256
COMBINE_ROWS = 256
ADA_COLS = 1024

VMEM_LIMIT = 56 * 1024 * 1024


def _params(n_axes, vmem=VMEM_LIMIT):
    return pltpu.CompilerParams(dimension_semantics=("arbitrary",) * n_axes, vmem_limit_bytes=vmem)


def _rms(x, g):
    return x * lax.rsqrt(jnp.mean(x * x, axis=-1, keepdims=True) + EPS) * g


def _gelu(x):
    return 0.5 * x * (1.0 + lax.erf(x * (2.0 ** -0.5)))


def _ada_body(c_ref, w_ref, b_ref, o_ref):
    c = c_ref[...]
    cond = c * jax.nn.sigmoid(c)
    o_ref[...] = jnp.dot(cond.astype(BF16), w_ref[...].astype(BF16), preferred_element_type=F32) + b_ref[...]


def _ada(c_pad, w_ada, b_ada):
    rows, d = c_pad.shape
    n = w_ada.shape[1]
    return pl.pallas_call(
        _ada_body,
        grid=(n // ADA_COLS,),
        in_specs=[pl.BlockSpec((rows, d), lambda j: (0, 0)),
                  pl.BlockSpec((d, ADA_COLS), lambda j: (0, j)),
                  pl.BlockSpec((1, ADA_COLS), lambda j: (0, j))],
        out_specs=pl.BlockSpec((rows, ADA_COLS), lambda j: (0, j)),
        out_shape=jax.ShapeDtypeStruct((rows, n), F32),
        compiler_params=_params(1),
        name="ada",
    )(c_pad, w_ada, b_ada)


def _pack_bf16_pair(lo, hi):
    lo_bits = pltpu.bitcast(lo.astype(BF16).astype(F32), U32)
    hi_bits = pltpu.bitcast(hi.astype(BF16).astype(F32), U32)
    return hi_bits | (lo_bits >> 16)


def _unpack_bf16_pair(p):
    lo = pltpu.bitcast(p << 16, F32).astype(BF16)
    hi = pltpu.bitcast(p & jnp.uint32(0xFFFF0000), F32).astype(BF16)
    return lo, hi


def _mix_body(x_ref, mod_ref, w2_ref, mixg_ref, win_ref, lng_ref, lnb_ref, ws_ref, bs_ref, pw_ref, pscale_ref,
              gog_ref, pog_ref, wout_ref, ffng_ref, rw_ref, rb_ref,
              x1_ref, h2p_ref, logit_ref, w2b_ref,
              pe_ref, ab_ref, cat_ref, *, tiles_per_seq):
    tr = x_ref.shape[0]
    w2b_ref[...] = w2_ref[...].astype(BF16)
    d = D_MODEL
    seq_tile = pl.program_id(0) % tiles_per_seq
    x = x_ref[...]
    shift_m = mod_ref[:, 0 * d:1 * d]
    scale_m = mod_ref[:, 1 * d:2 * d]
    gate_m = mod_ref[:, 2 * d:3 * d]
    shift_f = mod_ref[:, 3 * d:4 * d]
    scale_f = mod_ref[:, 4 * d:5 * d]

    hb = (_rms(x, mixg_ref[...]) * (1.0 + scale_m) + shift_m).astype(BF16)

    u = _gelu(jnp.dot(hb, win_ref[:, 0:GMLP_WIDTH], preferred_element_type=F32))
    v = _gelu(jnp.dot(hb, win_ref[:, GMLP_WIDTH:2 * GMLP_WIDTH], preferred_element_type=F32))
    slab = 2 * GMLP_BLOCK
    ri = lax.broadcasted_iota(I32, (slab, slab), 0)
    ci = lax.broadcasted_iota(I32, (slab, slab), 1)
    same_block = (ri // GMLP_BLOCK) == (ci // GMLP_BLOCK)
    causal = ((ri % GMLP_BLOCK) // CHUNK) >= ((ci % GMLP_BLOCK) // CHUNK)
    keep = same_block & causal
    for h in range(N_HEADS):
        sl = slice(h * HEAD_DIM, (h + 1) * HEAD_DIM)
        vh = v[:, sl]
        dv = vh - jnp.mean(vh, axis=-1, keepdims=True)
        var = jnp.mean(dv * dv, axis=-1, keepdims=True)
        vn = (dv * lax.rsqrt(var + EPS) * lng_ref[:, sl] + lnb_ref[:, sl]).astype(BF16)
        w_sp = jnp.where(keep, ws_ref[h], 0.0).astype(BF16)
        for s in range(tr // slab):
            rows = slice(s * slab, (s + 1) * slab)
            mixed = jnp.dot(w_sp, vn[rows], preferred_element_type=F32) + bs_ref[h]
            a = u[rows, sl] * mixed
            ab_ref[rows, sl] = a
    a_all = ab_ref[...]
    ssq_a = jnp.sum(a_all * a_all, axis=-1, keepdims=True)
    cat_ref[:, 0:GMLP_WIDTH] = (a_all * lax.rsqrt(ssq_a / GMLP_WIDTH + EPS) * gog_ref[...]).astype(BF16)

    p = jnp.dot(hb, win_ref[:, 2 * GMLP_WIDTH:], preferred_element_type=F32)

    @pl.when(seq_tile == 0)
    def _():
        pe_ref[0:POOL_HALO, :] = jnp.zeros((POOL_HALO, POOL_WIDTH), F32)

    pe_ref[POOL_HALO:, :] = p
    pos1 = (seq_tile * tr + lax.broadcasted_iota(I32, (tr, 1), 0) + 1).astype(F32)
    for g, w in enumerate(POOL_WINDOWS):
        cs = slice(g * POOL_GROUP_DIM, (g + 1) * POOL_GROUP_DIM)
        e = pe_ref[:, cs]
        s = e
        shift = 1
        while shift < w:
            s = s + pltpu.roll(s, shift, 0)
            shift *= 2
        inv = 1.0 / jnp.minimum(pos1, float(w))
        pooled = s[POOL_HALO:] * inv - e[POOL_HALO:]
        y = jnp.dot(pooled.astype(BF16), pw_ref[g], preferred_element_type=F32) * pscale_ref[:, cs]
        ab_ref[:, cs] = y
    pe_ref[0:POOL_HALO, :] = pe_ref[tr:tr + POOL_HALO, :]
    b_all = ab_ref[...]
    ssq_b = jnp.sum(b_all * b_all, axis=-1, keepdims=True)
    cat_ref[:, GMLP_WIDTH:] = (b_all * lax.rsqrt(ssq_b / POOL_WIDTH + EPS) * pog_ref[...]).astype(BF16)

    x1 = x + gate_m * jnp.dot(cat_ref[...], wout_ref[...], preferred_element_type=F32)
    x1_ref[...] = x1
    h2 = _rms(x1, ffng_ref[...]) * (1.0 + scale_f) + shift_f
    h2p_ref[...] = _pack_bf16_pair(h2[:, :HALF], h2[:, HALF:])
    h_hi = h2.astype(BF16)
    h_lo = (h2 - h_hi.astype(F32)).astype(BF16)
    l_hi = jnp.dot(h_hi, rw_ref[...], preferred_element_type=F32)
    l_lo = jnp.dot(h_lo, rw_ref[:, 0:LANES], preferred_element_type=F32)
    logit_ref[...] = l_hi[:, 0:LANES] + l_hi[:, LANES:] + l_lo + rb_ref[...]


def _const_spec(shape):
    nd = len(shape)
    return pl.BlockSpec(shape, lambda i: (0,) * nd, pipeline_mode=pl.Buffered(1))


def _mix(x2d, mod3, seq, w2, mix_g, w_in, ln_g, ln_b, ws2, bs2, pool_w, pool_scale, go_g, po_g, w_out, ffn_g, rw, rb):
    t, d = x2d.shape
    tr = MIX_ROWS
    tps = seq // tr
    n_e, f, _ = w2.shape
    slabs, rem = divmod(t // tr, n_e)
    assert rem == 0 and f % slabs == 0, "mix steps must split the expert weights evenly"
    slab_spec = pl.BlockSpec((None, f // slabs, d), lambda i: (i // slabs, i % slabs, 0))
    row_spec = lambda cols: pl.BlockSpec((tr, cols), lambda i: (i, 0))
    consts = [mix_g, w_in, ln_g, ln_b, ws2, bs2, pool_w, pool_scale, go_g, po_g, w_out, ffn_g, rw, rb]
    return pl.pallas_call(
        functools.partial(_mix_body, tiles_per_seq=tps),
        grid=(t // tr,),
        in_specs=[row_spec(d),
                  pl.BlockSpec((None, 1, mod3.shape[2]), lambda i: (i // tps, 0, 0)),
                  slab_spec]
                 + [_const_spec(a.shape) for a in consts],
        out_specs=[row_spec(d), row_spec(HALF), row_spec(LANES), slab_spec],
        out_shape=[jax.ShapeDtypeStruct((t, d), F32),
                   jax.ShapeDtypeStruct((t, HALF), U32),
                   jax.ShapeDtypeStruct((t, LANES), F32),
                   jax.ShapeDtypeStruct(w2.shape, BF16)],
        scratch_shapes=[pltpu.VMEM((tr + POOL_HALO, POOL_WIDTH), F32),
                        pltpu.VMEM((tr, GMLP_WIDTH), F32),
                        pltpu.VMEM((tr, d), BF16)],
        compiler_params=_params(1),
        name="mix",
    )(x2d, mod3, w2, *consts)


def _route_body(lg_ref, w1g_ref, dest_ref, gate_ref, cnt_ref, w1gb_ref, tot_ref, run_ref, start_ref):
    phase = pl.program_id(0)
    i = pl.program_id(1)
    tt = lg_ref.shape[0]
    w1gb_ref[...] = w1g_ref[...].astype(BF16)
    lane = lax.broadcasted_iota(I32, (tt, LANES), 1)
    l = jnp.where(lane < N_EXPERTS, lg_ref[...], -jnp.inf)
    sels, vals = [], []
    for _ in range(TOP_K):
        m = jnp.max(l, axis=1, keepdims=True)
        idx = jnp.min(jnp.where(l == m, lane, LANES), axis=1, keepdims=True)
        sel = lane == idx
        sels.append(sel)
        vals.append(m)
        l = jnp.where(sel, -jnp.inf, l)
    onehot = sels[0].astype(F32)
    for sel in sels[1:]:
        onehot = onehot + sel.astype(F32)
    colsum = jnp.sum(onehot, axis=0, keepdims=True)

    @pl.when((phase == 0) & (i == 0))
    def _():
        tot_ref[...] = jnp.zeros_like(tot_ref)

    @pl.when(phase == 0)
    def _():
        tot_ref[...] += colsum

    @pl.when((phase == 1) & (i == 0))
    def _():
        tot = tot_ref[...]
        padded = jnp.floor((tot + (SUB_ROWS - 1)) / SUB_ROWS) * SUB_ROWS
        r = lax.broadcasted_iota(I32, (LANES, LANES), 0)
        c = lax.broadcasted_iota(I32, (LANES, LANES), 1)
        col = jnp.sum(jnp.where(r == c, jnp.broadcast_to(padded, (LANES, LANES)), 0.0), axis=1, keepdims=True)
        start_ref[...] = jnp.sum(jnp.where(r < c, col, 0.0), axis=0, keepdims=True)
        run_ref[...] = jnp.zeros_like(run_ref)
        cnt_ref[...] = tot

    @pl.when(phase == 1)
    def _():
        r = lax.broadcasted_iota(I32, (tt, tt), 0)
        c = lax.broadcasted_iota(I32, (tt, tt), 1)
        earlier = (r > c).astype(BF16)
        prefix = jnp.dot(earlier, onehot.astype(BF16), preferred_element_type=F32)
        base = prefix + run_ref[...] + start_ref[...]
        dest = [jnp.sum(jnp.where(sel, base, 0.0), axis=1, keepdims=True) for sel in sels]
        dest_ref[...] = jnp.concatenate(dest, axis=1).astype(I32)
        ex = [jnp.exp(vk - vals[0]) for vk in vals]
        den = ex[0] + ex[1] + ex[2] + ex[3]
        gate_ref[...] = jnp.concatenate([e / den for e in ex], axis=1)
        run_ref[...] += colsum


def _route(logits, w1):
    t = logits.shape[0]
    tt = ROUTE_ROWS
    n_i = t // tt
    n_e, d, f2 = w1.shape
    nj = N_FCHUNKS
    assert 2 * n_i == n_e * nj and f2 == 2 * nj * F_CHUNK, "one up-projection chunk per route step"
    return pl.pallas_call(
        _route_body,
        grid=(2, n_i),
        in_specs=[pl.BlockSpec((tt, LANES), lambda p, i: (i, 0)),
                  pl.BlockSpec((None, d, F_CHUNK), lambda p, i: ((p * n_i + i) // nj, 0, (p * n_i + i) % nj))],
        out_specs=[pl.BlockSpec((tt, TOP_K), lambda p, i: (i * p, 0)),
                   pl.BlockSpec((tt, TOP_K), lambda p, i: (i * p, 0)),
                   pl.BlockSpec((1, LANES), lambda p, i: (0, 0)),
                   pl.BlockSpec((None, None, d, F_CHUNK),
                                lambda p, i: ((p * n_i + i) // nj, (p * n_i + i) % nj, 0, 0))],
        out_shape=[jax.ShapeDtypeStruct((t, TOP_K), I32),
                   jax.ShapeDtypeStruct((t, TOP_K), F32),
                   jax.ShapeDtypeStruct((1, LANES), F32),
                   jax.ShapeDtypeStruct((n_e, nj, d, F_CHUNK), BF16)],
        scratch_shapes=[pltpu.VMEM((1, LANES), F32)] * 3,
        compiler_params=_params(2),
        name="route",
    )(logits, w1)


def _dispatch_body(dest_sm, padfirst_sm, npad_sm, h2p_ref, w1l_ref, rows_ref, w1lb_ref, zero_ref, sem, zsem):
    td = h2p_ref.shape[0]
    step = pl.program_id(0)
    base = step * (td * TOP_K)
    w1lb_ref[...] = w1l_ref[...].astype(BF16)

    def issue(g, carry):
        t0 = pl.multiple_of(g * SUBLANES, SUBLANES)
        first = base + t0 * TOP_K
        for i in range(SUBLANES):
            for k in range(TOP_K):
                d = dest_sm[first + (i * TOP_K + k)]
                pltpu.make_async_copy(h2p_ref.at[pl.ds(t0 + i, 1), :], rows_ref.at[pl.ds(d, 1), :],
                                      sem).start(priority=k % 2)
        return carry

    lax.fori_loop(0, td // SUBLANES, issue, 0)

    @pl.when(step == 0)
    def _():
        zero_ref[...] = jnp.zeros_like(zero_ref)

        def zero_copy(r):
            return pltpu.make_async_copy(zero_ref, rows_ref.at[pl.ds(r, 1), :], zsem)

        def per_range(e, carry):
            first = padfirst_sm[e]
            n_pad = npad_sm[e]

            def start(r, c):
                zero_copy(first + r).start()
                return c

            def wait(r, c):
                zero_copy(first + r).wait()
                return c

            lax.fori_loop(0, n_pad, start, 0)
            lax.fori_loop(0, n_pad, wait, 0)
            return carry

        lax.fori_loop(0, N_EXPERTS + 1, per_range, 0)

    for k in range(TOP_K):
        pltpu.make_async_copy(h2p_ref, rows_ref.at[pl.ds(0, td), :], sem).wait()


def _dispatch(dest_flat, pad_first, n_pad, h2p, w1, n_rows):
    t = h2p.shape[0]
    td = DISPATCH_ROWS
    n_e, d, f2 = w1.shape
    nj = N_FCHUNKS
    assert t // td == n_e * nj and f2 == 2 * nj * F_CHUNK, "one up-projection chunk per dispatch step"
    return pl.pallas_call(
        _dispatch_body,
        grid_spec=pltpu.PrefetchScalarGridSpec(
            num_scalar_prefetch=3,
            grid=(t // td,),
            in_specs=[pl.BlockSpec((td, HALF), lambda i, *_: (i, 0)),
                      pl.BlockSpec((None, d, F_CHUNK), lambda i, *_: (i // nj, 0, nj + i % nj))],
            out_specs=[pl.BlockSpec(memory_space=pl.ANY),
                       pl.BlockSpec((None, None, d, F_CHUNK), lambda i, *_: (i // nj, i % nj, 0, 0))],
            scratch_shapes=[pltpu.VMEM((1, HALF), U32), pltpu.SemaphoreType.DMA, pltpu.SemaphoreType.DMA],
        ),
        out_shape=[jax.ShapeDtypeStruct((n_rows, HALF), U32),
                   jax.ShapeDtypeStruct((n_e, nj, d, F_CHUNK), BF16)],
        compiler_params=_params(1),
        name="dispatch",
    )(dest_flat, pad_first, n_pad, h2p, w1)


def _expert_body(vt, ve, vlo, vhi, vfirst, x_ref, w1g_ref, w1l_ref, b1g_ref, b1l_ref, w2_ref, b2_ref, o_ref):
    del vt, ve
    v = pl.program_id(0)
    j = pl.program_id(1)
    lo = vlo[v]
    hi = vhi[v]

    @pl.when((j == 0) & (vfirst[v] == 1))
    def _():
        o_ref[...] = jnp.zeros_like(o_ref)

    b2_first = jnp.where(j == 0, b2_ref[...], 0.0)

    def run(first_sub, n_sub):
        rows = pl.ds(pl.multiple_of(first_sub * SUB_ROWS, SUB_ROWS), n_sub * SUB_ROWS)
        x_lo, x_hi = _unpack_bf16_pair(x_ref[rows, :])
        for c in range(STEP_CHUNKS):
            cols = slice(c * F_CHUNK, (c + 1) * F_CHUNK)
            a_g = (jnp.dot(x_lo, w1g_ref[c, 0:HALF, :], preferred_element_type=F32)
                   + jnp.dot(x_hi, w1g_ref[c, HALF:, :], preferred_element_type=F32) + b1g_ref[:, cols])
            a_l = (jnp.dot(x_lo, w1l_ref[c, 0:HALF, :], preferred_element_type=F32)
                   + jnp.dot(x_hi, w1l_ref[c, HALF:, :], preferred_element_type=F32) + b1l_ref[:, cols])
            glu = jnp.minimum(a_g, SWIGLU_LIMIT)
            lin = jnp.clip(a_l, -SWIGLU_LIMIT, SWIGLU_LIMIT)
            act = glu * jax.nn.sigmoid(SWIGLU_ALPHA * glu) * (lin + 1.0)
            part = jnp.dot(act.astype(BF16), w2_ref[cols, :], preferred_element_type=F32)
            o_ref[rows, :] += (part + b2_first) if c == 0 else part

    n_sub = hi - lo

    @pl.when(n_sub == TILE_SUBS)
    def _():
        run(0, TILE_SUBS // 2)
        run(TILE_SUBS // 2, TILE_SUBS // 2)

    @pl.when((n_sub > 0) & (n_sub < TILE_SUBS))
    def _():
        def pair(p, carry):
            run(lo + 2 * p, 2)
            return carry

        lax.fori_loop(0, lax.shift_right_logical(n_sub, 1), pair, 0)

        @pl.when((n_sub & 1) == 1)
        def _():
            run(hi - 1, 1)


def _experts(tables, rows, w1gb, w1lb, b1, w2b, b2):
    n_rows = rows.shape[0]
    n_visits = tables[0].shape[0]
    nj = N_FSTEPS
    step_cols = STEP_CHUNKS * F_CHUNK

    def jeff(j, vlo, vhi, v):
        return jnp.where(vhi[v] > vlo[v], j, nj - 1)

    return pl.pallas_call(
        _expert_body,
        grid_spec=pltpu.PrefetchScalarGridSpec(
            num_scalar_prefetch=5,
            grid=(n_visits, nj),
            in_specs=[
                pl.BlockSpec((TILE_ROWS, HALF), lambda v, j, vt, ve, vlo, vhi, vf: (vt[v], 0)),
                pl.BlockSpec((None, STEP_CHUNKS, D_MODEL, F_CHUNK),
                             lambda v, j, vt, ve, vlo, vhi, vf: (ve[v], jeff(j, vlo, vhi, v), 0, 0)),
                pl.BlockSpec((None, STEP_CHUNKS, D_MODEL, F_CHUNK),
                             lambda v, j, vt, ve, vlo, vhi, vf: (ve[v], jeff(j, vlo, vhi, v), 0, 0)),
                pl.BlockSpec((None, 1, step_cols),
                             lambda v, j, vt, ve, vlo, vhi, vf: (ve[v], 0, jeff(j, vlo, vhi, v))),
                pl.BlockSpec((None, 1, step_cols),
                             lambda v, j, vt, ve, vlo, vhi, vf: (ve[v], 0, nj + jeff(j, vlo, vhi, v))),
                pl.BlockSpec((None, step_cols, D_MODEL),
                             lambda v, j, vt, ve, vlo, vhi, vf: (ve[v], jeff(j, vlo, vhi, v), 0)),
                pl.BlockSpec((None, 1, D_MODEL), lambda v, j, vt, ve, vlo, vhi, vf: (ve[v], 0, 0)),
            ],
            out_specs=pl.BlockSpec((TILE_ROWS, D_MODEL), lambda v, j, vt, ve, vlo, vhi, vf: (vt[v], 0)),
        ),
        out_shape=jax.ShapeDtypeStruct((n_rows, D_MODEL), F32),
        compiler_params=_params(2),
        name="experts",
    )(*tables, rows, w1gb, w1lb, b1, b1, w2b, b2)


def _layout_tables(counts, n_tiles):
    nblk = (counts + (SUB_ROWS - 1)) // SUB_ROWS
    blk_end = jnp.cumsum(nblk)
    blk_start = blk_end - nblk
    used_rows = blk_end[-1:] * SUB_ROWS
    pad_first = jnp.concatenate([blk_start * SUB_ROWS + counts, used_rows])
    n_pad = jnp.concatenate([nblk * SUB_ROWS - counts, n_tiles * TILE_ROWS - used_rows])

    n_visits = n_tiles + N_EXPERTS
    first_tile = blk_start // TILE_SUBS
    last_tile = (blk_end - 1) // TILE_SUBS
    nvis = jnp.where(nblk > 0, last_tile - first_tile + 1, 0)
    vis_end = jnp.cumsum(nvis)
    vis_start = vis_end - nvis
    total = vis_end[-1]
    v = jnp.arange(n_visits, dtype=I32)
    valid = v < total
    v_eff = jnp.minimum(v, total - 1)
    e = jnp.minimum(jnp.sum((vis_end[None, :] <= v_eff[:, None]).astype(I32), axis=1), N_EXPERTS - 1)
    onehot = (e[:, None] == jnp.arange(N_EXPERTS, dtype=I32)[None, :]).astype(I32)
    pick = lambda a: jnp.sum(onehot * a[None, :], axis=1)
    tile = jnp.minimum(pick(first_tile) + (v_eff - pick(vis_start)) + (v - v_eff), n_tiles - 1)
    lo = jnp.maximum(pick(blk_start), tile * TILE_SUBS) - tile * TILE_SUBS
    hi = jnp.minimum(pick(blk_end), (tile + 1) * TILE_SUBS) - tile * TILE_SUBS
    lo = jnp.where(valid, lo, 0)
    hi = jnp.where(valid, hi, 0)
    prev_tile = jnp.concatenate([jnp.full((1,), -1, I32), tile[:-1]])
    first = (tile != prev_tile).astype(I32)
    visit = tuple(a.astype(I32) for a in (tile, e, lo, hi, first))
    return pad_first.astype(I32), n_pad.astype(I32), visit


def _combine_body(dest_sm, rows_ref, x1_ref, gate_ref, mod_ref, fg_ref, o_ref, buf, sem, *, n_tiles):
    tc = x1_ref.shape[0]
    i = pl.program_id(0)

    def row_copy(d, slot, k, t):
        return pltpu.make_async_copy(rows_ref.at[pl.ds(d, 1), :], buf.at[slot, k, pl.ds(t, 1), :], sem.at[slot])

    def issue(tile, slot):
        base = tile * (tc * TOP_K)

        def one(g, carry):
            t0 = pl.multiple_of(g * SUBLANES, SUBLANES)
            first = base + t0 * TOP_K
            for i in range(SUBLANES):
                for k in range(TOP_K):
                    row_copy(dest_sm[first + (i * TOP_K + k)], slot, k, t0 + i).start(priority=k % 2)
            return carry

        lax.fori_loop(0, tc // SUBLANES, one, 0)

    @pl.when(i == 0)
    def _():
        issue(0, 0)

    @pl.when(i + 1 < n_tiles)
    def _():
        issue(i + 1, (i + 1) % 2)

    slot = i % 2

    for k in range(TOP_K):
        pltpu.make_async_copy(rows_ref.at[pl.ds(0, tc), :], buf.at[slot, k], sem.at[slot]).wait()

    g = gate_ref[...]
    y = g[:, 0:1] * buf[slot, 0]
    for k in range(1, TOP_K):
        y = y + g[:, k:k + 1] * buf[slot, k]
    gate_f = mod_ref[:, 5 * D_MODEL:6 * D_MODEL]
    o_ref[...] = _rms(x1_ref[...] + gate_f * y, fg_ref[...])


def _combine(dest_flat, out_rows, x1, gates, mod3, seq, final_g):
    t, d = x1.shape
    tc = COMBINE_ROWS
    tps = seq // tc
    n_tiles = t // tc
    return pl.pallas_call(
        functools.partial(_combine_body, n_tiles=n_tiles),
        grid_spec=pltpu.PrefetchScalarGridSpec(
            num_scalar_prefetch=1,
            grid=(n_tiles,),
            in_specs=[pl.BlockSpec(memory_space=pl.ANY),
                      pl.BlockSpec((tc, d), lambda i, s: (i, 0)),
                      pl.BlockSpec((tc, TOP_K), lambda i, s: (i, 0)),
                      pl.BlockSpec((None, 1, mod3.shape[2]), lambda i, s: (i // tps, 0, 0)),
                      pl.BlockSpec((1, d), lambda i, s: (0, 0))],
            out_specs=pl.BlockSpec((tc, d), lambda i, s: (i, 0)),
            scratch_shapes=[pltpu.VMEM((2, TOP_K, tc, d), F32),
                            pltpu.SemaphoreType.DMA((2,))],
        ),
        out_shape=jax.ShapeDtypeStruct((t, d), F32),
        compiler_params=_params(1),
        name="combine",
    )(dest_flat, out_rows, x1, gates, mod3, final_g)


def kernel(x, c, mix_norm_g, w_ada, b_ada, w_in, gmlp_ln_g, gmlp_ln_b, gmlp_ws, gmlp_bs, pool_w, pool_scale,
           gmlp_out_g, pool_out_g, w_out, ffn_norm_g, router_w, router_b, moe_w1, moe_b1, moe_w2, moe_b2,
           final_norm_g):
    bsz, seq, d = x.shape
    t = bsz * seq
    assert d == D_MODEL and w_ada.shape[0] == 1, "single-layer block with d_model 2048"
    assert seq % MIX_ROWS == 0 and seq % COMBINE_ROWS == 0 and t % ROUTE_ROWS == 0 and t % DISPATCH_ROWS == 0
    row = lambda a: a.reshape(1, -1)

    c_pad = jnp.zeros((8, d), F32).at[:bsz].set(c)
    mod3 = _ada(c_pad, w_ada[0], row(b_ada[0]))[:bsz].reshape(bsz, 1, 6 * d)

    ws2 = jnp.tile(gmlp_ws[0], (1, 2, 2))
    bs2 = jnp.tile(gmlp_bs[0], (1, 2))[:, :, None]
    rw_hi = router_w[0].astype(BF16)
    rw_lo = (router_w[0] - rw_hi.astype(F32)).astype(BF16)
    lane_pad = lambda a: jnp.pad(a, ((0, 0), (0, LANES - a.shape[1])))
    rw = jnp.concatenate([lane_pad(rw_hi), lane_pad(rw_lo)], axis=1)
    rb = lane_pad(row(router_b[0]))

    x1, h2p, logits, w2b = _mix(
        x.reshape(t, d), mod3, seq, moe_w2[0], row(mix_norm_g[0]), w_in[0].astype(BF16), row(gmlp_ln_g[0]),
        row(gmlp_ln_b[0]), ws2, bs2, pool_w[0].astype(BF16), row(pool_scale[0]), row(gmlp_out_g[0]),
        row(pool_out_g[0]), w_out[0].astype(BF16), row(ffn_norm_g[0]), rw, rb)

    dest, gates, counts, w1gb = _route(logits, moe_w1[0])
    dest_flat = dest.reshape(t * TOP_K)

    n_asg = t * TOP_K
    n_rows = -(-(n_asg + N_EXPERTS * SUB_ROWS) // TILE_ROWS) * TILE_ROWS
    pad_first, n_pad, tables = _layout_tables(counts[0, :N_EXPERTS].astype(I32), n_rows // TILE_ROWS)
    rows, w1lb = _dispatch(dest_flat, pad_first, n_pad, h2p, moe_w1[0], n_rows)
    e, f = N_EXPERTS, EXPERT_DIM
    out_rows = _experts(tables, rows, w1gb, w1lb, moe_b1[0].reshape(e, 1, 2 * f), w2b, moe_b2[0].reshape(e, 1, d))

    y = _combine(dest_flat, out_rows, x1, gates, mod3, seq, row(final_norm_g))
    return y.reshape(bsz, seq, d)
```

```python
import functools

import jax
import jax.numpy as jnp
from jax import lax
from jax.experimental import pallas as pl
from jax.experimental.pallas import tpu as pltpu

F32 = jnp.float32
BF16 = jnp.bfloat16
I32 = jnp.int32
U32 = jnp.uint32

D_MODEL = 2048
GMLP_WIDTH = 1024
HEAD_DIM = 128
N_HEADS = GMLP_WIDTH // HEAD_DIM
GMLP_BLOCK = 128
CHUNK = 64
POOL_WIDTH = 1024
POOL_WINDOWS = (2, 4, 8, 16)
POOL_GROUP_DIM = POOL_WIDTH // len(POOL_WINDOWS)
POOL_HALO = 16
N_EXPERTS = 32
TOP_K = 4
EXPERT_DIM = 2048
SWIGLU_ALPHA = 1.702
SWIGLU_LIMIT = 7.0
EPS = 1e-5

LANES = 128
SUBLANES = 8
SUB_ROWS = 256
TILE_SUBS = 4
TILE_ROWS = SUB_ROWS * TILE_SUBS
F_CHUNK = 512
N_FCHUNKS = EXPERT_DIM // F_CHUNK
STEP_CHUNKS = 2
N_FSTEPS = N_FCHUNKS // STEP_CHUNKS
HALF = D_MODEL // 2

MIX_ROWS = 256
ROUTE_ROWS = 512
DISPATCH_ROWS = 256
COMBINE_ROWS = 256
ADA_COLS = 1024

VMEM_LIMIT = 56 * 1024 * 1024


def _params(n_axes, vmem=VMEM_LIMIT):
    return pltpu.CompilerParams(dimension_semantics=("arbitrary",) * n_axes, vmem_limit_bytes=vmem)


def _rms(x, g):
    return x * lax.rsqrt(jnp.mean(x * x, axis=-1, keepdims=True) + EPS) * g


def _gelu(x):
    return 0.5 * x * (1.0 + lax.erf(x * (2.0 ** -0.5)))


def _ada_body(c_ref, w_ref, b_ref, o_ref):
    c = c_ref[...]
    cond = c * jax.nn.sigmoid(c)
    o_ref[...] = jnp.dot(cond.astype(BF16), w_ref[...].astype(BF16), preferred_element_type=F32) + b_ref[...]


def _ada(c_pad, w_ada, b_ada):
    rows, d = c_pad.shape
    n = w_ada.shape[1]
    return pl.pallas_call(
        _ada_body,
        grid=(n // ADA_COLS,),
        in_specs=[pl.BlockSpec((rows, d), lambda j: (0, 0)),
                  pl.BlockSpec((d, ADA_COLS), lambda j: (0, j)),
                  pl.BlockSpec((1, ADA_COLS), lambda j: (0, j))],
        out_specs=pl.BlockSpec((rows, ADA_COLS), lambda j: (0, j)),
        out_shape=jax.ShapeDtypeStruct((rows, n), F32),
        compiler_params=_params(1),
        name="ada",
    )(c_pad, w_ada, b_ada)


def _pack_bf16_pair(lo, hi):
    lo_bits = pltpu.bitcast(lo.astype(BF16).astype(F32), U32)
    hi_bits = pltpu.bitcast(hi.astype(BF16).astype(F32), U32)
    return hi_bits | (lo_bits >> 16)


def _unpack_bf16_pair(p):
    lo = pltpu.bitcast(p << 16, F32).astype(BF16)
    hi = pltpu.bitcast(p & jnp.uint32(0xFFFF0000), F32).astype(BF16)
    return lo, hi


def _mix_body(x_ref, mod_ref, w2_ref, mixg_ref, win_ref, lng_ref, lnb_ref, ws_ref, bs_ref, pw_ref, pscale_ref,
              gog_ref, pog_ref, wout_ref, ffng_ref, rw_ref, rb_ref,
              x1_ref, h2p_ref, logit_ref, w2b_ref,
              pe_ref, ab_ref, cat_ref, *, tiles_per_seq):
    tr = x_ref.shape[0]
    w2b_ref[...] = w2_ref[...].astype(BF16)
    d = D_MODEL
    seq_tile = pl.program_id(0) % tiles_per_seq
    x = x_ref[...]
    shift_m = mod_ref[:, 0 * d:1 * d]
    scale_m = mod_ref[:, 1 * d:2 * d]
    gate_m = mod_ref[:, 2 * d:3 * d]
    shift_f = mod_ref[:, 3 * d:4 * d]
    scale_f = mod_ref[:, 4 * d:5 * d]

    hb = (_rms(x, mixg_ref[...]) * (1.0 + scale_m) + shift_m).astype(BF16)

    u = _gelu(jnp.dot(hb, win_ref[:, 0:GMLP_WIDTH], preferred_element_type=F32))
    v = _gelu(jnp.dot(hb, win_ref[:, GMLP_WIDTH:2 * GMLP_WIDTH], preferred_element_type=F32))
    slab = 2 * GMLP_BLOCK
    ri = lax.broadcasted_iota(I32, (slab, slab), 0)
    ci = lax.broadcasted_iota(I32, (slab, slab), 1)
    same_block = (ri // GMLP_BLOCK) == (ci // GMLP_BLOCK)
    causal = ((ri % GMLP_BLOCK) // CHUNK) >= ((ci % GMLP_BLOCK) // CHUNK)
    keep = same_block & causal
    for h in range(N_HEADS):
        sl = slice(h * HEAD_DIM, (h + 1) * HEAD_DIM)
        vh = v[:, sl]
        dv = vh - jnp.mean(vh, axis=-1, keepdims=True)
        var = jnp.mean(dv * dv, axis=-1, keepdims=True)
        vn = (dv * lax.rsqrt(var + EPS) * lng_ref[:, sl] + lnb_ref[:, sl]).astype(BF16)
        w_sp = jnp.where(keep, ws_ref[h], 0.0).astype(BF16)
        for s in range(tr // slab):
            rows = slice(s * slab, (s + 1) * slab)
            mixed = jnp.dot(w_sp, vn[rows], preferred_element_type=F32) + bs_ref[h]
            a = u[rows, sl] * mixed
            ab_ref[rows, sl] = a
    a_all = ab_ref[...]
    ssq_a = jnp.sum(a_all * a_all, axis=-1, keepdims=True)
    cat_ref[:, 0:GMLP_WIDTH] = (a_all * lax.rsqrt(ssq_a / GMLP_WIDTH + EPS) * gog_ref[...]).astype(BF16)

    p = jnp.dot(hb, win_ref[:, 2 * GMLP_WIDTH:], preferred_element_type=F32)

    @pl.when(seq_tile == 0)
    def _():
        pe_ref[0:POOL_HALO, :] = jnp.zeros((POOL_HALO, POOL_WIDTH), F32)

    pe_ref[POOL_HALO:, :] = p
    pos1 = (seq_tile * tr + lax.broadcasted_iota(I32, (tr, 1), 0) + 1).astype(F32)
    for g, w in enumerate(POOL_WINDOWS):
        cs = slice(g * POOL_GROUP_DIM, (g + 1) * POOL_GROUP_DIM)
        e = pe_ref[:, cs]
        s = e
        shift = 1
        while shift < w:
            s = s + pltpu.roll(s, shift, 0)
            shift *= 2
        inv = 1.0 / jnp.minimum(pos1, float(w))
        pooled = s[POOL_HALO:] * inv - e[POOL_HALO:]
        y = jnp.dot(pooled.astype(BF16), pw_ref[g], preferred_element_type=F32) * pscale_ref[:, cs]
        ab_ref[:, cs] = y
    pe_ref[0:POOL_HALO, :] = pe_ref[tr:tr + POOL_HALO, :]
    b_all = ab_ref[...]
    ssq_b = jnp.sum(b_all * b_all, axis=-1, keepdims=True)
    cat_ref[:, GMLP_WIDTH:] = (b_all * lax.rsqrt(ssq_b / POOL_WIDTH + EPS) * pog_ref[...]).astype(BF16)

    x1 = x + gate_m * jnp.dot(cat_ref[...], wout_ref[...], preferred_element_type=F32)
    x1_ref[...] = x1
    h2 = _rms(x1, ffng_ref[...]) * (1.0 + scale_f) + shift_f
    h2p_ref[...] = _pack_bf16_pair(h2[:, :HALF], h2[:, HALF:])
    h_hi = h2.astype(BF16)
    h_lo = (h2 - h_hi.astype(F32)).astype(BF16)
    l_hi = jnp.dot(h_hi, rw_ref[...], preferred_element_type=F32)
    l_lo = jnp.dot(h_lo, rw_ref[:, 0:LANES], preferred_element_type=F32)
    logit_ref[...] = l_hi[:, 0:LANES] + l_hi[:, LANES:] + l_lo + rb_ref[...]


def _const_spec(shape):
    nd = len(shape)
    return pl.BlockSpec(shape, lambda i: (0,) * nd, pipeline_mode=pl.Buffered(1))


def _mix(x2d, mod3, seq, w2, mix_g, w_in, ln_g, ln_b, ws2, bs2, pool_w, pool_scale, go_g, po_g, w_out, ffn_g, rw, rb):
    t, d = x2d.shape
    tr = MIX_ROWS
    tps = seq // tr
    n_e, f, _ = w2.shape
    slabs, rem = divmod(t // tr, n_e)
    assert rem == 0 and f % slabs == 0, "mix steps must split the expert weights evenly"
    slab_spec = pl.BlockSpec((None, f // slabs, d), lambda i: (i // slabs, i % slabs, 0))
    row_spec = lambda cols: pl.BlockSpec((tr, cols), lambda i: (i, 0))
    consts = [mix_g, w_in, ln_g, ln_b, ws2, bs2, pool_w, pool_scale, go_g, po_g, w_out, ffn_g, rw, rb]
    return pl.pallas_call(
        functools.partial(_mix_body, tiles_per_seq=tps),
        grid=(t // tr,),
        in_specs=[row_spec(d),
                  pl.BlockSpec((None, 1, mod3.shape[2]), lambda i: (i // tps, 0, 0)),
                  slab_spec]
                 + [_const_spec(a.shape) for a in consts],
        out_specs=[row_spec(d), row_spec(HALF), row_spec(LANES), slab_spec],
        out_shape=[jax.ShapeDtypeStruct((t, d), F32),
                   jax.ShapeDtypeStruct((t, HALF), U32),
                   jax.ShapeDtypeStruct((t, LANES), F32),
                   jax.ShapeDtypeStruct(w2.shape, BF16)],
        scratch_shapes=[pltpu.VMEM((tr + POOL_HALO, POOL_WIDTH), F32),
                        pltpu.VMEM((tr, GMLP_WIDTH), F32),
                        pltpu.VMEM((tr, d), BF16)],
        compiler_params=_params(1),
        name="mix",
    )(x2d, mod3, w2, *consts)


def _route_body(lg_ref, w1g_ref, dest_ref, gate_ref, cnt_ref, w1gb_ref, tot_ref, run_ref, start_ref):
    phase = pl.program_id(0)
    i = pl.program_id(1)
    tt = lg_ref.shape[0]
    w1gb_ref[...] = w1g_ref[...].astype(BF16)
    lane = lax.broadcasted_iota(I32, (tt, LANES), 1)
    l = jnp.where(lane < N_EXPERTS, lg_ref[...], -jnp.inf)
    sels, vals = [], []
    for _ in range(TOP_K):
        m = jnp.max(l, axis=1, keepdims=True)
        idx = jnp.min(jnp.where(l == m, lane, LANES), axis=1, keepdims=True)
        sel = lane == idx
        sels.append(sel)
        vals.append(m)
        l = jnp.where(sel, -jnp.inf, l)
    onehot = sels[0].astype(F32)
    for sel in sels[1:]:
        onehot = onehot + sel.astype(F32)
    colsum = jnp.sum(onehot, axis=0, keepdims=True)

    @pl.when((phase == 0) & (i == 0))
    def _():
        tot_ref[...] = jnp.zeros_like(tot_ref)

    @pl.when(phase == 0)
    def _():
        tot_ref[...] += colsum

    @pl.when((phase == 1) & (i == 0))
    def _():
        tot = tot_ref[...]
        padded = jnp.floor((tot + (SUB_ROWS - 1)) / SUB_ROWS) * SUB_ROWS
        r = lax.broadcasted_iota(I32, (LANES, LANES), 0)
        c = lax.broadcasted_iota(I32, (LANES, LANES), 1)
        col = jnp.sum(jnp.where(r == c, jnp.broadcast_to(padded, (LANES, LANES)), 0.0), axis=1, keepdims=True)
        start_ref[...] = jnp.sum(jnp.where(r < c, col, 0.0), axis=0, keepdims=True)
        run_ref[...] = jnp.zeros_like(run_ref)
        cnt_ref[...] = tot

    @pl.when(phase == 1)
    def _():
        r = lax.broadcasted_iota(I32, (tt, tt), 0)
        c = lax.broadcasted_iota(I32, (tt, tt), 1)
        earlier = (r > c).astype(BF16)
        prefix = jnp.dot(earlier, onehot.astype(BF16), preferred_element_type=F32)
        base = prefix + run_ref[...] + start_ref[...]
        dest = [jnp.sum(jnp.where(sel, base, 0.0), axis=1, keepdims=True) for sel in sels]
        dest_ref[...] = jnp.concatenate(dest, axis=1).astype(I32)
        ex = [jnp.exp(vk - vals[0]) for vk in vals]
        den = ex[0] + ex[1] + ex[2] + ex[3]
        gate_ref[...] = jnp.concatenate([e / den for e in ex], axis=1)
        run_ref[...] += colsum


def _route(logits, w1):
    t = logits.shape[0]
    tt = ROUTE_ROWS
    n_i = t // tt
    n_e, d, f2 = w1.shape
    nj = N_FCHUNKS
    assert 2 * n_i == n_e * nj and f2 == 2 * nj * F_CHUNK, "one up-projection chunk per route step"
    return pl.pallas_call(
        _route_body,
        grid=(2, n_i),
        in_specs=[pl.BlockSpec((tt, LANES), lambda p, i: (i, 0)),
                  pl.BlockSpec((None, d, F_CHUNK), lambda p, i: ((p * n_i + i) // nj, 0, (p * n_i + i) % nj))],
        out_specs=[pl.BlockSpec((tt, TOP_K), lambda p, i: (i * p, 0)),
                   pl.BlockSpec((tt, TOP_K), lambda p, i: (i * p, 0)),
                   pl.BlockSpec((1, LANES), lambda p, i: (0, 0)),
                   pl.BlockSpec((None, None, d, F_CHUNK),
                                lambda p, i: ((p * n_i + i) // nj, (p * n_i + i) % nj, 0, 0))],
        out_shape=[jax.ShapeDtypeStruct((t, TOP_K), I32),
                   jax.ShapeDtypeStruct((t, TOP_K), F32),
                   jax.ShapeDtypeStruct((1, LANES), F32),
                   jax.ShapeDtypeStruct((n_e, nj, d, F_CHUNK), BF16)],
        scratch_shapes=[pltpu.VMEM((1, LANES), F32)] * 3,
        compiler_params=_params(2),
        name="route",
    )(logits, w1)


def _dispatch_body(dest_sm, padfirst_sm, npad_sm, h2p_ref, w1l_ref, rows_ref, w1lb_ref, zero_ref, sem, zsem):
    td = h2p_ref.shape[0]
    step = pl.program_id(0)
    base = step * (td * TOP_K)
    w1lb_ref[...] = w1l_ref[...].astype(BF16)

    def issue(g, carry):
        t0 = pl.multiple_of(g * SUBLANES, SUBLANES)
        first = base + t0 * TOP_K
        for i in range(SUBLANES):
            for k in range(TOP_K):
                d = dest_sm[first + (i * TOP_K + k)]
                pltpu.make_async_copy(h2p_ref.at[pl.ds(t0 + i, 1), :], rows_ref.at[pl.ds(d, 1), :],
                                      sem).start(priority=k % 2)
        return carry

    lax.fori_loop(0, td // SUBLANES, issue, 0)

    @pl.when(step == 0)
    def _():
        zero_ref[...] = jnp.zeros_like(zero_ref)

        def zero_copy(r):
            return pltpu.make_async_copy(zero_ref, rows_ref.at[pl.ds(r, 1), :], zsem)

        def per_range(e, carry):
            first = padfirst_sm[e]
            n_pad = npad_sm[e]

            def start(r, c):
                zero_copy(first + r).start()
                return c

            def wait(r, c):
                zero_copy(first + r).wait()
                return c

            lax.fori_loop(0, n_pad, start, 0)
            lax.fori_loop(0, n_pad, wait, 0)
            return carry

        lax.fori_loop(0, N_EXPERTS + 1, per_range, 0)

    for k in range(TOP_K):
        pltpu.make_async_copy(h2p_ref, rows_ref.at[pl.ds(0, td), :], sem).wait()


def _dispatch(dest_flat, pad_first, n_pad, h2p, w1, n_rows):
    t = h2p.shape[0]
    td = DISPATCH_ROWS
    n_e, d, f2 = w1.shape
    nj = N_FCHUNKS
    assert t // td == n_e * nj and f2 == 2 * nj * F_CHUNK, "one up-projection chunk per dispatch step"
    return pl.pallas_call(
        _dispatch_body,
        grid_spec=pltpu.PrefetchScalarGridSpec(
            num_scalar_prefetch=3,
            grid=(t // td,),
            in_specs=[pl.BlockSpec((td, HALF), lambda i, *_: (i, 0)),
                      pl.BlockSpec((None, d, F_CHUNK), lambda i, *_: (i // nj, 0, nj + i % nj))],
            out_specs=[pl.BlockSpec(memory_space=pl.ANY),
                       pl.BlockSpec((None, None, d, F_CHUNK), lambda i, *_: (i // nj, i % nj, 0, 0))],
            scratch_shapes=[pltpu.VMEM((1, HALF), U32), pltpu.SemaphoreType.DMA, pltpu.SemaphoreType.DMA],
        ),
        out_shape=[jax.ShapeDtypeStruct((n_rows, HALF), U32),
                   jax.ShapeDtypeStruct((n_e, nj, d, F_CHUNK), BF16)],
        compiler_params=_params(1),
        name="dispatch",
    )(dest_flat, pad_first, n_pad, h2p, w1)


def _expert_body(vt, ve, vlo, vhi, vfirst, x_ref, w1g_ref, w1l_ref, b1g_ref, b1l_ref, w2_ref, b2_ref, o_ref):
    del vt, ve
    v = pl.program_id(0)
    j = pl.program_id(1)
    lo = vlo[v]
    hi = vhi[v]

    n_sub = hi - lo

    @pl.when((j == 0) & (vfirst[v] == 1) & (n_sub < TILE_SUBS))
    def _():
        o_ref[...] = jnp.zeros_like(o_ref)

    b2_first = jnp.where(j == 0, b2_ref[...], 0.0)

    def run(first_sub, n_sub, mode="select"):
        rows = pl.ds(pl.multiple_of(first_sub * SUB_ROWS, SUB_ROWS), n_sub * SUB_ROWS)
        x_lo, x_hi = _unpack_bf16_pair(x_ref[rows, :])
        for c in range(STEP_CHUNKS):
            cols = slice(c * F_CHUNK, (c + 1) * F_CHUNK)
            a_g = (jnp.dot(x_lo, w1g_ref[c, 0:HALF, :], preferred_element_type=F32)
                   + jnp.dot(x_hi, w1g_ref[c, HALF:, :], preferred_element_type=F32) + b1g_ref[:, cols])
            a_l = (jnp.dot(x_lo, w1l_ref[c, 0:HALF, :], preferred_element_type=F32)
                   + jnp.dot(x_hi, w1l_ref[c, HALF:, :], preferred_element_type=F32) + b1l_ref[:, cols])
            glu = jnp.minimum(a_g, SWIGLU_LIMIT)
            lin = jnp.clip(a_l, -SWIGLU_LIMIT, SWIGLU_LIMIT)
            act = glu * jax.nn.sigmoid(SWIGLU_ALPHA * glu) * (lin + 1.0)
            part = jnp.dot(act.astype(BF16), w2_ref[cols, :], preferred_element_type=F32)
            if mode == "assign" and c == 0:
                o_ref[rows, :] = part + b2_ref[...]
            elif mode == "select" and c == 0:
                o_ref[rows, :] += part + b2_first
            else:
                o_ref[rows, :] += part

    @pl.when((n_sub == TILE_SUBS) & (j == 0))
    def _():
        run(0, TILE_SUBS // 2, "assign")
        run(TILE_SUBS // 2, TILE_SUBS // 2, "assign")

    @pl.when((n_sub == TILE_SUBS) & (j > 0))
    def _():
        run(0, TILE_SUBS // 2, "accumulate")
        run(TILE_SUBS // 2, TILE_SUBS // 2, "accumulate")

    @pl.when((n_sub > 0) & (n_sub < TILE_SUBS))
    def _():
        def pair(p, carry):
            run(lo + 2 * p, 2)
            return carry

        lax.fori_loop(0, lax.shift_right_logical(n_sub, 1), pair, 0)

        @pl.when((n_sub & 1) == 1)
        def _():
            run(hi - 1, 1)


def _experts(tables, rows, w1gb, w1lb, b1, w2b, b2):
    n_rows = rows.shape[0]
    n_visits = tables[0].shape[0]
    nj = N_FSTEPS
    step_cols = STEP_CHUNKS * F_CHUNK

    def jeff(j, vlo, vhi, v):
        return jnp.where(vhi[v] > vlo[v], j, nj - 1)

    return pl.pallas_call(
        _expert_body,
        grid_spec=pltpu.PrefetchScalarGridSpec(
            num_scalar_prefetch=5,
            grid=(n_visits, nj),
            in_specs=[
                pl.BlockSpec((TILE_ROWS, HALF), lambda v, j, vt, ve, vlo, vhi, vf: (vt[v], 0)),
                pl.BlockSpec((None, STEP_CHUNKS, D_MODEL, F_CHUNK),
                             lambda v, j, vt, ve, vlo, vhi, vf: (ve[v], jeff(j, vlo, vhi, v), 0, 0)),
                pl.BlockSpec((None, STEP_CHUNKS, D_MODEL, F_CHUNK),
                             lambda v, j, vt, ve, vlo, vhi, vf: (ve[v], jeff(j, vlo, vhi, v), 0, 0)),
                pl.BlockSpec((None, 1, step_cols),
                             lambda v, j, vt, ve, vlo, vhi, vf: (ve[v], 0, jeff(j, vlo, vhi, v))),
                pl.BlockSpec((None, 1, step_cols),
                             lambda v, j, vt, ve, vlo, vhi, vf: (ve[v], 0, nj + jeff(j, vlo, vhi, v))),
                pl.BlockSpec((None, step_cols, D_MODEL),
                             lambda v, j, vt, ve, vlo, vhi, vf: (ve[v], jeff(j, vlo, vhi, v), 0)),
                pl.BlockSpec((None, 1, D_MODEL), lambda v, j, vt, ve, vlo, vhi, vf: (ve[v], 0, 0)),
            ],
            out_specs=pl.BlockSpec((TILE_ROWS, D_MODEL), lambda v, j, vt, ve, vlo, vhi, vf: (vt[v], 0)),
        ),
        out_shape=jax.ShapeDtypeStruct((n_rows, D_MODEL), F32),
        compiler_params=_params(2),
        name="experts",
    )(*tables, rows, w1gb, w1lb, b1, b1, w2b, b2)


def _layout_tables(counts, n_tiles):
    nblk = (counts + (SUB_ROWS - 1)) // SUB_ROWS
    blk_end = jnp.cumsum(nblk)
    blk_start = blk_end - nblk
    used_rows = blk_end[-1:] * SUB_ROWS
    pad_first = jnp.concatenate([blk_start * SUB_ROWS + counts, used_rows])
    n_pad = jnp.concatenate([nblk * SUB_ROWS - counts, n_tiles * TILE_ROWS - used_rows])

    n_visits = n_tiles + N_EXPERTS
    first_tile = blk_start // TILE_SUBS
    last_tile = (blk_end - 1) // TILE_SUBS
    nvis = jnp.where(nblk > 0, last_tile - first_tile + 1, 0)
    vis_end = jnp.cumsum(nvis)
    vis_start = vis_end - nvis
    total = vis_end[-1]
    v = jnp.arange(n_visits, dtype=I32)
    valid = v < total
    v_eff = jnp.minimum(v, total - 1)
    e = jnp.minimum(jnp.sum((vis_end[None, :] <= v_eff[:, None]).astype(I32), axis=1), N_EXPERTS - 1)
    onehot = (e[:, None] == jnp.arange(N_EXPERTS, dtype=I32)[None, :]).astype(I32)
    pick = lambda a: jnp.sum(onehot * a[None, :], axis=1)
    tile = jnp.minimum(pick(first_tile) + (v_eff - pick(vis_start)) + (v - v_eff), n_tiles - 1)
    lo = jnp.maximum(pick(blk_start), tile * TILE_SUBS) - tile * TILE_SUBS
    hi = jnp.minimum(pick(blk_end), (tile + 1) * TILE_SUBS) - tile * TILE_SUBS
    lo = jnp.where(valid, lo, 0)
    hi = jnp.where(valid, hi, 0)
    prev_tile = jnp.concatenate([jnp.full((1,), -1, I32), tile[:-1]])
    first = (tile != prev_tile).astype(I32)
    visit = tuple(a.astype(I32) for a in (tile, e, lo, hi, first))
    return pad_first.astype(I32), n_pad.astype(I32), visit


def _combine_body(dest_sm, rows_ref, x1_ref, gate_ref, mod_ref, fg_ref, o_ref, buf, sem, *, n_tiles):
    tc = x1_ref.shape[0]
    i = pl.program_id(0)

    def row_copy(d, slot, k, t):
        return pltpu.make_async_copy(rows_ref.at[pl.ds(d, 1), :], buf.at[slot, k, pl.ds(t, 1), :], sem.at[slot])

    def issue(tile, slot):
        base = tile * (tc * TOP_K)

        def one(g, carry):
            t0 = pl.multiple_of(g * SUBLANES, SUBLANES)
            first = base + t0 * TOP_K
            for i in range(SUBLANES):
                for k in range(TOP_K):
                    row_copy(dest_sm[first + (i * TOP_K + k)], slot, k, t0 + i).start(priority=k % 2)
            return carry

        lax.fori_loop(0, tc // SUBLANES, one, 0)

    @pl.when(i == 0)
    def _():
        issue(0, 0)

    @pl.when(i + 1 < n_tiles)
    def _():
        issue(i + 1, (i + 1) % 2)

    slot = i % 2

    for k in range(TOP_K):
        pltpu.make_async_copy(rows_ref.at[pl.ds(0, tc), :], buf.at[slot, k], sem.at[slot]).wait()

    g = gate_ref[...]
    y = g[:, 0:1] * buf[slot, 0]
    for k in range(1, TOP_K):
        y = y + g[:, k:k + 1] * buf[slot, k]
    gate_f = mod_ref[:, 5 * D_MODEL:6 * D_MODEL]
    o_ref[...] = _rms(x1_ref[...] + gate_f * y, fg_ref[...])


def _combine(dest_flat, out_rows, x1, gates, mod3, seq, final_g):
    t, d = x1.shape
    tc = COMBINE_ROWS
    tps = seq // tc
    n_tiles = t // tc
    return pl.pallas_call(
        functools.partial(_combine_body, n_tiles=n_tiles),
        grid_spec=pltpu.PrefetchScalarGridSpec(
            num_scalar_prefetch=1,
            grid=(n_tiles,),
            in_specs=[pl.BlockSpec(memory_space=pl.ANY),
                      pl.BlockSpec((tc, d), lambda i, s: (i, 0)),
                      pl.BlockSpec((tc, TOP_K), lambda i, s: (i, 0)),
                      pl.BlockSpec((None, 1, mod3.shape[2]), lambda i, s: (i // tps, 0, 0)),
                      pl.BlockSpec((1, d), lambda i, s: (0, 0))],
            out_specs=pl.BlockSpec((tc, d), lambda i, s: (i, 0)),
            scratch_shapes=[pltpu.VMEM((2, TOP_K, tc, d), F32),
                            pltpu.SemaphoreType.DMA((2,))],
        ),
        out_shape=jax.ShapeDtypeStruct((t, d), F32),
        compiler_params=_params(1),
        name="combine",
    )(dest_flat, out_rows, x1, gates, mod3, final_g)


def kernel(x, c, mix_norm_g, w_ada, b_ada, w_in, gmlp_ln_g, gmlp_ln_b, gmlp_ws, gmlp_bs, pool_w, pool_scale,
           gmlp_out_g, pool_out_g, w_out, ffn_norm_g, router_w, router_b, moe_w1, moe_b1, moe_w2, moe_b2,
           final_norm_g):
    bsz, seq, d = x.shape
    t = bsz * seq
    assert d == D_MODEL and w_ada.shape[0] == 1, "single-layer block with d_model 2048"
    assert seq % MIX_ROWS == 0 and seq % COMBINE_ROWS == 0 and t % ROUTE_ROWS == 0 and t % DISPATCH_ROWS == 0
    row = lambda a: a.reshape(1, -1)

    c_pad = jnp.zeros((8, d), F32).at[:bsz].set(c)
    mod3 = _ada(c_pad, w_ada[0], row(b_ada[0]))[:bsz].reshape(bsz, 1, 6 * d)

    ws2 = jnp.tile(gmlp_ws[0], (1, 2, 2))
    bs2 = jnp.tile(gmlp_bs[0], (1, 2))[:, :, None]
    rw_hi = router_w[0].astype(BF16)
    rw_lo = (router_w[0] - rw_hi.astype(F32)).astype(BF16)
    lane_pad = lambda a: jnp.pad(a, ((0, 0), (0, LANES - a.shape[1])))
    rw = jnp.concatenate([lane_pad(rw_hi), lane_pad(rw_lo)], axis=1)
    rb = lane_pad(row(router_b[0]))

    x1, h2p, logits, w2b = _mix(
        x.reshape(t, d), mod3, seq, moe_w2[0], row(mix_norm_g[0]), w_in[0].astype(BF16), row(gmlp_ln_g[0]),
        row(gmlp_ln_b[0]), ws2, bs2, pool_w[0].astype(BF16), row(pool_scale[0]), row(gmlp_out_g[0]),
        row(pool_out_g[0]), w_out[0].astype(BF16), row(ffn_norm_g[0]), rw, rb)

    dest, gates, counts, w1gb = _route(logits, moe_w1[0])
    dest_flat = dest.reshape(t * TOP_K)

    n_asg = t * TOP_K
    n_rows = -(-(n_asg + N_EXPERTS * SUB_ROWS) // TILE_ROWS) * TILE_ROWS
    pad_first, n_pad, tables = _layout_tables(counts[0, :N_EXPERTS].astype(I32), n_rows // TILE_ROWS)
    rows, w1lb = _dispatch(dest_flat, pad_first, n_pad, h2p, moe_w1[0], n_rows)
    e, f = N_EXPERTS, EXPERT_DIM
    out_rows = _experts(tables, rows, w1gb, w1lb, moe_b1[0].reshape(e, 1, 2 * f), w2b, moe_b2[0].reshape(e, 1, d))

    y = _combine(dest_flat, out_rows, x1, gates, mod3, seq, row(final_norm_g))
    return y.reshape(bsz, seq, d)
```

```python
import functools

import jax
import jax.numpy as jnp
from jax import lax
from jax.experimental import pallas as pl
from jax.experimental.pallas import tpu as pltpu

F32 = jnp.float32
BF16 = jnp.bfloat16
I32 = jnp.int32
U32 = jnp.uint32

D_MODEL = 2048
GMLP_WIDTH = 1024
HEAD_DIM = 128
N_HEADS = GMLP_WIDTH // HEAD_DIM
GMLP_BLOCK = 128
CHUNK = 64
POOL_WIDTH = 1024
POOL_WINDOWS = (2, 4, 8, 16)
POOL_GROUP_DIM = POOL_WIDTH // len(POOL_WINDOWS)
POOL_HALO = 16
N_EXPERTS = 32
TOP_K = 4
EXPERT_DIM = 2048
SWIGLU_ALPHA = 1.702
SWIGLU_LIMIT = 7.0
EPS = 1e-5

LANES = 128
SUBLANES = 8
SUB_ROWS = 256
TILE_SUBS = 4
TILE_ROWS = SUB_ROWS * TILE_SUBS
F_CHUNK = 512
N_FCHUNKS = EXPERT_DIM // F_CHUNK
STEP_CHUNKS = 2
N_FSTEPS = N_FCHUNKS // STEP_CHUNKS
HALF = D_MODEL // 2

MIX_ROWS = 256
ROUTE_ROWS = 512
DISPATCH_ROWS = 256
COMBINE_ROWS = 256
ADA_COLS = 1024

VMEM_LIMIT = 58 * 1024 * 1024


def _params(n_axes, vmem=VMEM_LIMIT):
    return pltpu.CompilerParams(dimension_semantics=("arbitrary",) * n_axes, vmem_limit_bytes=vmem)


def _rms(x, g):
    return x * lax.rsqrt(jnp.mean(x * x, axis=-1, keepdims=True) + EPS) * g


def _gelu(x):
    return 0.5 * x * (1.0 + lax.erf(x * (2.0 ** -0.5)))


def _ada_body(c_ref, w_ref, b_ref, o_ref):
    c = c_ref[...]
    cond = c * jax.nn.sigmoid(c)
    o_ref[...] = jnp.dot(cond.astype(BF16), w_ref[...].astype(BF16), preferred_element_type=F32) + b_ref[...]


def _ada(c_pad, w_ada, b_ada):
    rows, d = c_pad.shape
    n = w_ada.shape[1]
    return pl.pallas_call(
        _ada_body,
        grid=(n // ADA_COLS,),
        in_specs=[pl.BlockSpec((rows, d), lambda j: (0, 0)),
                  pl.BlockSpec((d, ADA_COLS), lambda j: (0, j)),
                  pl.BlockSpec((1, ADA_COLS), lambda j: (0, j))],
        out_specs=pl.BlockSpec((rows, ADA_COLS), lambda j: (0, j)),
        out_shape=jax.ShapeDtypeStruct((rows, n), F32),
        compiler_params=_params(1),
        name="ada",
    )(c_pad, w_ada, b_ada)


def _pack_bf16_pair(lo, hi):
    lo_bits = pltpu.bitcast(lo.astype(BF16).astype(F32), U32)
    hi_bits = pltpu.bitcast(hi.astype(BF16).astype(F32), U32)
    return hi_bits | (lo_bits >> 16)


def _unpack_bf16_pair(p):
    lo = pltpu.bitcast(p << 16, F32).astype(BF16)
    hi = pltpu.bitcast(p & jnp.uint32(0xFFFF0000), F32).astype(BF16)
    return lo, hi


def _mix_body(x_ref, mod_ref, w2_ref, mixg_ref, win_ref, lng_ref, lnb_ref, ws_ref, bs_ref, pw_ref, pscale_ref,
              gog_ref, pog_ref, wout_ref, ffng_ref, rw_ref, rb_ref,
              x1_ref, h2p_ref, logit_ref, w2b_ref,
              pe_ref, ab_ref, cat_ref, *, tiles_per_seq):
    tr = x_ref.shape[0]
    w2b_ref[...] = w2_ref[...].astype(BF16)
    d = D_MODEL
    seq_tile = pl.program_id(0) % tiles_per_seq
    x = x_ref[...]
    shift_m = mod_ref[:, 0 * d:1 * d]
    scale_m = mod_ref[:, 1 * d:2 * d]
    gate_m = mod_ref[:, 2 * d:3 * d]
    shift_f = mod_ref[:, 3 * d:4 * d]
    scale_f = mod_ref[:, 4 * d:5 * d]

    hb = (_rms(x, mixg_ref[...]) * (1.0 + scale_m) + shift_m).astype(BF16)

    z = jnp.dot(hb, win_ref[...], preferred_element_type=F32)
    u = _gelu(z[:, 0:GMLP_WIDTH])
    v = _gelu(z[:, GMLP_WIDTH:2 * GMLP_WIDTH])
    slab = 2 * GMLP_BLOCK
    ri = lax.broadcasted_iota(I32, (slab, slab), 0)
    ci = lax.broadcasted_iota(I32, (slab, slab), 1)
    same_block = (ri // GMLP_BLOCK) == (ci // GMLP_BLOCK)
    causal = ((ri % GMLP_BLOCK) // CHUNK) >= ((ci % GMLP_BLOCK) // CHUNK)
    keep = same_block & causal
    for h in range(N_HEADS):
        sl = slice(h * HEAD_DIM, (h + 1) * HEAD_DIM)
        vh = v[:, sl]
        dv = vh - jnp.mean(vh, axis=-1, keepdims=True)
        var = jnp.mean(dv * dv, axis=-1, keepdims=True)
        vn = (dv * lax.rsqrt(var + EPS) * lng_ref[:, sl] + lnb_ref[:, sl]).astype(BF16)
        w_sp = jnp.where(keep, ws_ref[h], 0.0).astype(BF16)
        for s in range(tr // slab):
            rows = slice(s * slab, (s + 1) * slab)
            mixed = jnp.dot(w_sp, vn[rows], preferred_element_type=F32) + bs_ref[h]
            a = u[rows, sl] * mixed
            ab_ref[rows, sl] = a
    a_all = ab_ref[...]
    ssq_a = jnp.sum(a_all * a_all, axis=-1, keepdims=True)
    cat_ref[:, 0:GMLP_WIDTH] = (a_all * lax.rsqrt(ssq_a / GMLP_WIDTH + EPS) * gog_ref[...]).astype(BF16)

    p = z[:, 2 * GMLP_WIDTH:]

    @pl.when(seq_tile == 0)
    def _():
        pe_ref[0:POOL_HALO, :] = jnp.zeros((POOL_HALO, POOL_WIDTH), F32)

    pe_ref[POOL_HALO:, :] = p
    pos1 = (seq_tile * tr + lax.broadcasted_iota(I32, (tr, 1), 0) + 1).astype(F32)
    for g, w in enumerate(POOL_WINDOWS):
        cs = slice(g * POOL_GROUP_DIM, (g + 1) * POOL_GROUP_DIM)
        e = pe_ref[:, cs]
        s = e
        shift = 1
        while shift < w:
            s = s + pltpu.roll(s, shift, 0)
            shift *= 2
        inv = 1.0 / jnp.minimum(pos1, float(w))
        pooled = s[POOL_HALO:] * inv - e[POOL_HALO:]
        y = jnp.dot(pooled.astype(BF16), pw_ref[g], preferred_element_type=F32) * pscale_ref[:, cs]
        ab_ref[:, cs] = y
    pe_ref[0:POOL_HALO, :] = pe_ref[tr:tr + POOL_HALO, :]
    b_all = ab_ref[...]
    ssq_b = jnp.sum(b_all * b_all, axis=-1, keepdims=True)
    cat_ref[:, GMLP_WIDTH:] = (b_all * lax.rsqrt(ssq_b / POOL_WIDTH + EPS) * pog_ref[...]).astype(BF16)

    x1 = x + gate_m * jnp.dot(cat_ref[...], wout_ref[...], preferred_element_type=F32)
    x1_ref[...] = x1
    h2 = _rms(x1, ffng_ref[...]) * (1.0 + scale_f) + shift_f
    h2p_ref[...] = _pack_bf16_pair(h2[:, :HALF], h2[:, HALF:])
    h_hi = h2.astype(BF16)
    h_lo = (h2 - h_hi.astype(F32)).astype(BF16)
    l_hi = jnp.dot(h_hi, rw_ref[...], preferred_element_type=F32)
    l_lo = jnp.dot(h_lo, rw_ref[:, 0:LANES], preferred_element_type=F32)
    logit_ref[...] = l_hi[:, 0:LANES] + l_hi[:, LANES:] + l_lo + rb_ref[...]


def _const_spec(shape):
    nd = len(shape)
    return pl.BlockSpec(shape, lambda i: (0,) * nd, pipeline_mode=pl.Buffered(1))


def _mix(x2d, mod3, seq, w2, mix_g, w_in, ln_g, ln_b, ws2, bs2, pool_w, pool_scale, go_g, po_g, w_out, ffn_g, rw, rb):
    t, d = x2d.shape
    tr = MIX_ROWS
    tps = seq // tr
    n_e, f, _ = w2.shape
    slabs, rem = divmod(t // tr, n_e)
    assert rem == 0 and f % slabs == 0, "mix steps must split the expert weights evenly"
    slab_spec = pl.BlockSpec((None, f // slabs, d), lambda i: (i // slabs, i % slabs, 0))
    row_spec = lambda cols: pl.BlockSpec((tr, cols), lambda i: (i, 0))
    consts = [mix_g, w_in, ln_g, ln_b, ws2, bs2, pool_w, pool_scale, go_g, po_g, w_out, ffn_g, rw, rb]
    return pl.pallas_call(
        functools.partial(_mix_body, tiles_per_seq=tps),
        grid=(t // tr,),
        in_specs=[row_spec(d),
                  pl.BlockSpec((None, 1, mod3.shape[2]), lambda i: (i // tps, 0, 0)),
                  slab_spec]
                 + [_const_spec(a.shape) for a in consts],
        out_specs=[row_spec(d), row_spec(HALF), row_spec(LANES), slab_spec],
        out_shape=[jax.ShapeDtypeStruct((t, d), F32),
                   jax.ShapeDtypeStruct((t, HALF), U32),
                   jax.ShapeDtypeStruct((t, LANES), F32),
                   jax.ShapeDtypeStruct(w2.shape, BF16)],
        scratch_shapes=[pltpu.VMEM((tr + POOL_HALO, POOL_WIDTH), F32),
                        pltpu.VMEM((tr, GMLP_WIDTH), F32),
                        pltpu.VMEM((tr, d), BF16)],
        compiler_params=_params(1),
        name="mix",
    )(x2d, mod3, w2, *consts)


def _route_body(lg_ref, w1g_ref, dest_ref, gate_ref, cnt_ref, w1gb_ref, tot_ref, run_ref, start_ref):
    phase = pl.program_id(0)
    i = pl.program_id(1)
    tt = lg_ref.shape[0]
    w1gb_ref[...] = w1g_ref[...].astype(BF16)
    lane = lax.broadcasted_iota(I32, (tt, LANES), 1)
    l = jnp.where(lane < N_EXPERTS, lg_ref[...], -jnp.inf)
    sels, vals = [], []
    for _ in range(TOP_K):
        m = jnp.max(l, axis=1, keepdims=True)
        idx = jnp.min(jnp.where(l == m, lane, LANES), axis=1, keepdims=True)
        sel = lane == idx
        sels.append(sel)
        vals.append(m)
        l = jnp.where(sel, -jnp.inf, l)
    onehot = sels[0].astype(F32)
    for sel in sels[1:]:
        onehot = onehot + sel.astype(F32)
    colsum = jnp.sum(onehot, axis=0, keepdims=True)

    @pl.when((phase == 0) & (i == 0))
    def _():
        tot_ref[...] = jnp.zeros_like(tot_ref)

    @pl.when(phase == 0)
    def _():
        tot_ref[...] += colsum

    @pl.when((phase == 1) & (i == 0))
    def _():
        tot = tot_ref[...]
        padded = jnp.floor((tot + (SUB_ROWS - 1)) / SUB_ROWS) * SUB_ROWS
        r = lax.broadcasted_iota(I32, (LANES, LANES), 0)
        c = lax.broadcasted_iota(I32, (LANES, LANES), 1)
        col = jnp.sum(jnp.where(r == c, jnp.broadcast_to(padded, (LANES, LANES)), 0.0), axis=1, keepdims=True)
        start_ref[...] = jnp.sum(jnp.where(r < c, col, 0.0), axis=0, keepdims=True)
        run_ref[...] = jnp.zeros_like(run_ref)
        cnt_ref[...] = tot

    @pl.when(phase == 1)
    def _():
        r = lax.broadcasted_iota(I32, (tt, tt), 0)
        c = lax.broadcasted_iota(I32, (tt, tt), 1)
        earlier = (r > c).astype(BF16)
        prefix = jnp.dot(earlier, onehot.astype(BF16), preferred_element_type=F32)
        base = prefix + run_ref[...] + start_ref[...]
        dest = [jnp.sum(jnp.where(sel, base, 0.0), axis=1, keepdims=True) for sel in sels]
        dest_ref[...] = jnp.concatenate(dest, axis=1).astype(I32)
        ex = [jnp.exp(vk - vals[0]) for vk in vals]
        den = ex[0] + ex[1] + ex[2] + ex[3]
        gate_ref[...] = jnp.concatenate([e / den for e in ex], axis=1)
        run_ref[...] += colsum


def _route(logits, w1):
    t = logits.shape[0]
    tt = ROUTE_ROWS
    n_i = t // tt
    n_e, d, f2 = w1.shape
    nj = N_FCHUNKS
    assert 2 * n_i == n_e * nj and f2 == 2 * nj * F_CHUNK, "one up-projection chunk per route step"
    return pl.pallas_call(
        _route_body,
        grid=(2, n_i),
        in_specs=[pl.BlockSpec((tt, LANES), lambda p, i: (i, 0)),
                  pl.BlockSpec((None, d, F_CHUNK), lambda p, i: ((p * n_i + i) // nj, 0, (p * n_i + i) % nj))],
        out_specs=[pl.BlockSpec((tt, TOP_K), lambda p, i: (i * p, 0)),
                   pl.BlockSpec((tt, TOP_K), lambda p, i: (i * p, 0)),
                   pl.BlockSpec((1, LANES), lambda p, i: (0, 0)),
                   pl.BlockSpec((None, None, d, F_CHUNK),
                                lambda p, i: ((p * n_i + i) // nj, (p * n_i + i) % nj, 0, 0))],
        out_shape=[jax.ShapeDtypeStruct((t, TOP_K), I32),
                   jax.ShapeDtypeStruct((t, TOP_K), F32),
                   jax.ShapeDtypeStruct((1, LANES), F32),
                   jax.ShapeDtypeStruct((n_e, nj, d, F_CHUNK), BF16)],
        scratch_shapes=[pltpu.VMEM((1, LANES), F32)] * 3,
        compiler_params=_params(2),
        name="route",
    )(logits, w1)


def _dispatch_body(dest_sm, padfirst_sm, npad_sm, h2p_ref, w1l_ref, rows_ref, w1lb_ref, zero_ref, sem, zsem):
    td = h2p_ref.shape[0]
    step = pl.program_id(0)
    base = step * (td * TOP_K)
    w1lb_ref[...] = w1l_ref[...].astype(BF16)

    def issue(g, carry):
        t0 = pl.multiple_of(g * SUBLANES, SUBLANES)
        first = base + t0 * TOP_K
        for i in range(SUBLANES):
            for k in range(TOP_K):
                d = dest_sm[first + (i * TOP_K + k)]
                pltpu.make_async_copy(h2p_ref.at[pl.ds(t0 + i, 1), :], rows_ref.at[pl.ds(d, 1), :],
                                      sem).start(priority=k % 2)
        return carry

    lax.fori_loop(0, td // SUBLANES, issue, 0)

    @pl.when(step == 0)
    def _():
        zero_ref[...] = jnp.zeros_like(zero_ref)

        def zero_copy(r):
            return pltpu.make_async_copy(zero_ref, rows_ref.at[pl.ds(r, 1), :], zsem)

        def per_range(e, carry):
            first = padfirst_sm[e]
            n_pad = npad_sm[e]

            def start(r, c):
                zero_copy(first + r).start()
                return c

            def wait(r, c):
                zero_copy(first + r).wait()
                return c

            lax.fori_loop(0, n_pad, start, 0)
            lax.fori_loop(0, n_pad, wait, 0)
            return carry

        lax.fori_loop(0, N_EXPERTS + 1, per_range, 0)

    for k in range(TOP_K):
        pltpu.make_async_copy(h2p_ref, rows_ref.at[pl.ds(0, td), :], sem).wait()


def _dispatch(dest_flat, pad_first, n_pad, h2p, w1, n_rows):
    t = h2p.shape[0]
    td = DISPATCH_ROWS
    n_e, d, f2 = w1.shape
    nj = N_FCHUNKS
    assert t // td == n_e * nj and f2 == 2 * nj * F_CHUNK, "one up-projection chunk per dispatch step"
    return pl.pallas_call(
        _dispatch_body,
        grid_spec=pltpu.PrefetchScalarGridSpec(
            num_scalar_prefetch=3,
            grid=(t // td,),
            in_specs=[pl.BlockSpec((td, HALF), lambda i, *_: (i, 0)),
                      pl.BlockSpec((None, d, F_CHUNK), lambda i, *_: (i // nj, 0, nj + i % nj))],
            out_specs=[pl.BlockSpec(memory_space=pl.ANY),
                       pl.BlockSpec((None, None, d, F_CHUNK), lambda i, *_: (i // nj, i % nj, 0, 0))],
            scratch_shapes=[pltpu.VMEM((1, HALF), U32), pltpu.SemaphoreType.DMA, pltpu.SemaphoreType.DMA],
        ),
        out_shape=[jax.ShapeDtypeStruct((n_rows, HALF), U32),
                   jax.ShapeDtypeStruct((n_e, nj, d, F_CHUNK), BF16)],
        compiler_params=_params(1),
        name="dispatch",
    )(dest_flat, pad_first, n_pad, h2p, w1)


def _expert_body(vt, ve, vlo, vhi, vfirst, x_ref, w1g_ref, w1l_ref, b1g_ref, b1l_ref, w2_ref, b2_ref, o_ref):
    del vt, ve
    v = pl.program_id(0)
    j = pl.program_id(1)
    lo = vlo[v]
    hi = vhi[v]

    n_sub = hi - lo

    @pl.when((j == 0) & (vfirst[v] == 1) & (n_sub < TILE_SUBS))
    def _():
        o_ref[...] = jnp.zeros_like(o_ref)

    b2_first = jnp.where(j == 0, b2_ref[...], 0.0)

    def run(first_sub, n_sub, mode="select"):
        rows = pl.ds(pl.multiple_of(first_sub * SUB_ROWS, SUB_ROWS), n_sub * SUB_ROWS)
        x_lo, x_hi = _unpack_bf16_pair(x_ref[rows, :])
        for c in range(STEP_CHUNKS):
            cols = slice(c * F_CHUNK, (c + 1) * F_CHUNK)
            a_g = (jnp.dot(x_lo, w1g_ref[c, 0:HALF, :], preferred_element_type=F32)
                   + jnp.dot(x_hi, w1g_ref[c, HALF:, :], preferred_element_type=F32) + b1g_ref[:, cols])
            a_l = (jnp.dot(x_lo, w1l_ref[c, 0:HALF, :], preferred_element_type=F32)
                   + jnp.dot(x_hi, w1l_ref[c, HALF:, :], preferred_element_type=F32) + b1l_ref[:, cols])
            glu = jnp.minimum(a_g, SWIGLU_LIMIT)
            lin = jnp.clip(a_l, -SWIGLU_LIMIT, SWIGLU_LIMIT)
            act = glu * jax.nn.sigmoid(SWIGLU_ALPHA * glu) * (lin + 1.0)
            part = jnp.dot(act.astype(BF16), w2_ref[cols, :], preferred_element_type=F32)
            if mode == "assign" and c == 0:
                o_ref[rows, :] = part + b2_ref[...]
            elif mode == "select" and c == 0:
                o_ref[rows, :] += part + b2_first
            else:
                o_ref[rows, :] += part

    @pl.when((n_sub == TILE_SUBS) & (j == 0))
    def _():
        run(0, TILE_SUBS // 2, "assign")
        run(TILE_SUBS // 2, TILE_SUBS // 2, "assign")

    @pl.when((n_sub == TILE_SUBS) & (j > 0))
    def _():
        run(0, TILE_SUBS // 2, "accumulate")
        run(TILE_SUBS // 2, TILE_SUBS // 2, "accumulate")

    @pl.when((n_sub > 0) & (n_sub < TILE_SUBS))
    def _():
        def pair(p, carry):
            run(lo + 2 * p, 2)
            return carry

        lax.fori_loop(0, lax.shift_right_logical(n_sub, 1), pair, 0)

        @pl.when((n_sub & 1) == 1)
        def _():
            run(hi - 1, 1)


def _experts(tables, rows, w1gb, w1lb, b1, w2b, b2):
    n_rows = rows.shape[0]
    n_visits = tables[0].shape[0]
    nj = N_FSTEPS
    step_cols = STEP_CHUNKS * F_CHUNK

    def jeff(j, vlo, vhi, v):
        return jnp.where(vhi[v] > vlo[v], j, nj - 1)

    return pl.pallas_call(
        _expert_body,
        grid_spec=pltpu.PrefetchScalarGridSpec(
            num_scalar_prefetch=5,
            grid=(n_visits, nj),
            in_specs=[
                pl.BlockSpec((TILE_ROWS, HALF), lambda v, j, vt, ve, vlo, vhi, vf: (vt[v], 0)),
                pl.BlockSpec((None, STEP_CHUNKS, D_MODEL, F_CHUNK),
                             lambda v, j, vt, ve, vlo, vhi, vf: (ve[v], jeff(j, vlo, vhi, v), 0, 0)),
                pl.BlockSpec((None, STEP_CHUNKS, D_MODEL, F_CHUNK),
                             lambda v, j, vt, ve, vlo, vhi, vf: (ve[v], jeff(j, vlo, vhi, v), 0, 0)),
                pl.BlockSpec((None, 1, step_cols),
                             lambda v, j, vt, ve, vlo, vhi, vf: (ve[v], 0, jeff(j, vlo, vhi, v))),
                pl.BlockSpec((None, 1, step_cols),
                             lambda v, j, vt, ve, vlo, vhi, vf: (ve[v], 0, nj + jeff(j, vlo, vhi, v))),
                pl.BlockSpec((None, step_cols, D_MODEL),
                             lambda v, j, vt, ve, vlo, vhi, vf: (ve[v], jeff(j, vlo, vhi, v), 0)),
                pl.BlockSpec((None, 1, D_MODEL), lambda v, j, vt, ve, vlo, vhi, vf: (ve[v], 0, 0)),
            ],
            out_specs=pl.BlockSpec((TILE_ROWS, D_MODEL), lambda v, j, vt, ve, vlo, vhi, vf: (vt[v], 0)),
        ),
        out_shape=jax.ShapeDtypeStruct((n_rows, D_MODEL), F32),
        compiler_params=_params(2),
        name="experts",
    )(*tables, rows, w1gb, w1lb, b1, b1, w2b, b2)


def _layout_tables(counts, n_tiles):
    nblk = (counts + (SUB_ROWS - 1)) // SUB_ROWS
    blk_end = jnp.cumsum(nblk)
    blk_start = blk_end - nblk
    used_rows = blk_end[-1:] * SUB_ROWS
    pad_first = jnp.concatenate([blk_start * SUB_ROWS + counts, used_rows])
    n_pad = jnp.concatenate([nblk * SUB_ROWS - counts, n_tiles * TILE_ROWS - used_rows])

    n_visits = n_tiles + N_EXPERTS
    first_tile = blk_start // TILE_SUBS
    last_tile = (blk_end - 1) // TILE_SUBS
    nvis = jnp.where(nblk > 0, last_tile - first_tile + 1, 0)
    vis_end = jnp.cumsum(nvis)
    vis_start = vis_end - nvis
    total = vis_end[-1]
    v = jnp.arange(n_visits, dtype=I32)
    valid = v < total
    v_eff = jnp.minimum(v, total - 1)
    e = jnp.minimum(jnp.sum((vis_end[None, :] <= v_eff[:, None]).astype(I32), axis=1), N_EXPERTS - 1)
    onehot = (e[:, None] == jnp.arange(N_EXPERTS, dtype=I32)[None, :]).astype(I32)
    pick = lambda a: jnp.sum(onehot * a[None, :], axis=1)
    tile = jnp.minimum(pick(first_tile) + (v_eff - pick(vis_start)) + (v - v_eff), n_tiles - 1)
    lo = jnp.maximum(pick(blk_start), tile * TILE_SUBS) - tile * TILE_SUBS
    hi = jnp.minimum(pick(blk_end), (tile + 1) * TILE_SUBS) - tile * TILE_SUBS
    lo = jnp.where(valid, lo, 0)
    hi = jnp.where(valid, hi, 0)
    prev_tile = jnp.concatenate([jnp.full((1,), -1, I32), tile[:-1]])
    first = (tile != prev_tile).astype(I32)
    visit = tuple(a.astype(I32) for a in (tile, e, lo, hi, first))
    return pad_first.astype(I32), n_pad.astype(I32), visit


def _combine_body(dest_sm, rows_ref, x1_ref, gate_ref, mod_ref, fg_ref, o_ref, buf, sem, *, n_tiles):
    tc = x1_ref.shape[0]
    i = pl.program_id(0)

    def row_copy(d, slot, k, t):
        return pltpu.make_async_copy(rows_ref.at[pl.ds(d, 1), :], buf.at[slot, k, pl.ds(t, 1), :], sem.at[slot])

    def issue(tile, slot):
        base = tile * (tc * TOP_K)

        def one(g, carry):
            t0 = pl.multiple_of(g * SUBLANES, SUBLANES)
            first = base + t0 * TOP_K
            for i in range(SUBLANES):
                for k in range(TOP_K):
                    row_copy(dest_sm[first + (i * TOP_K + k)], slot, k, t0 + i).start(priority=k % 2)
            return carry

        lax.fori_loop(0, tc // SUBLANES, one, 0)

    @pl.when(i == 0)
    def _():
        issue(0, 0)

    @pl.when(i + 1 < n_tiles)
    def _():
        issue(i + 1, (i + 1) % 2)

    slot = i % 2

    for k in range(TOP_K):
        pltpu.make_async_copy(rows_ref.at[pl.ds(0, tc), :], buf.at[slot, k], sem.at[slot]).wait()

    g = gate_ref[...]
    y = g[:, 0:1] * buf[slot, 0]
    for k in range(1, TOP_K):
        y = y + g[:, k:k + 1] * buf[slot, k]
    gate_f = mod_ref[:, 5 * D_MODEL:6 * D_MODEL]
    o_ref[...] = _rms(x1_ref[...] + gate_f * y, fg_ref[...])


def _combine(dest_flat, out_rows, x1, gates, mod3, seq, final_g):
    t, d = x1.shape
    tc = COMBINE_ROWS
    tps = seq // tc
    n_tiles = t // tc
    return pl.pallas_call(
        functools.partial(_combine_body, n_tiles=n_tiles),
        grid_spec=pltpu.PrefetchScalarGridSpec(
            num_scalar_prefetch=1,
            grid=(n_tiles,),
            in_specs=[pl.BlockSpec(memory_space=pl.ANY),
                      pl.BlockSpec((tc, d), lambda i, s: (i, 0)),
                      pl.BlockSpec((tc, TOP_K), lambda i, s: (i, 0)),
                      pl.BlockSpec((None, 1, mod3.shape[2]), lambda i, s: (i // tps, 0, 0)),
                      pl.BlockSpec((1, d), lambda i, s: (0, 0))],
            out_specs=pl.BlockSpec((tc, d), lambda i, s: (i, 0)),
            scratch_shapes=[pltpu.VMEM((2, TOP_K, tc, d), F32),
                            pltpu.SemaphoreType.DMA((2,))],
        ),
        out_shape=jax.ShapeDtypeStruct((t, d), F32),
        compiler_params=_params(1),
        name="combine",
    )(dest_flat, out_rows, x1, gates, mod3, final_g)


def kernel(x, c, mix_norm_g, w_ada, b_ada, w_in, gmlp_ln_g, gmlp_ln_b, gmlp_ws, gmlp_bs, pool_w, pool_scale,
           gmlp_out_g, pool_out_g, w_out, ffn_norm_g, router_w, router_b, moe_w1, moe_b1, moe_w2, moe_b2,
           final_norm_g):
    bsz, seq, d = x.shape
    t = bsz * seq
    assert d == D_MODEL and w_ada.shape[0] == 1, "single-layer block with d_model 2048"
    assert seq % MIX_ROWS == 0 and seq % COMBINE_ROWS == 0 and t % ROUTE_ROWS == 0 and t % DISPATCH_ROWS == 0
    row = lambda a: a.reshape(1, -1)

    c_pad = jnp.zeros((8, d), F32).at[:bsz].set(c)
    mod3 = _ada(c_pad, w_ada[0], row(b_ada[0]))[:bsz].reshape(bsz, 1, 6 * d)

    ws2 = jnp.tile(gmlp_ws[0], (1, 2, 2))
    bs2 = jnp.tile(gmlp_bs[0], (1, 2))[:, :, None]
    rw_hi = router_w[0].astype(BF16)
    rw_lo = (router_w[0] - rw_hi.astype(F32)).astype(BF16)
    lane_pad = lambda a: jnp.pad(a, ((0, 0), (0, LANES - a.shape[1])))
    rw = jnp.concatenate([lane_pad(rw_hi), lane_pad(rw_lo)], axis=1)
    rb = lane_pad(row(router_b[0]))

    x1, h2p, logits, w2b = _mix(
        x.reshape(t, d), mod3, seq, moe_w2[0], row(mix_norm_g[0]), w_in[0].astype(BF16), row(gmlp_ln_g[0]),
        row(gmlp_ln_b[0]), ws2, bs2, pool_w[0].astype(BF16), row(pool_scale[0]), row(gmlp_out_g[0]),
        row(pool_out_g[0]), w_out[0].astype(BF16), row(ffn_norm_g[0]), rw, rb)

    dest, gates, counts, w1gb = _route(logits, moe_w1[0])
    dest_flat = dest.reshape(t * TOP_K)

    n_asg = t * TOP_K
    n_rows = -(-(n_asg + N_EXPERTS * SUB_ROWS) // TILE_ROWS) * TILE_ROWS
    pad_first, n_pad, tables = _layout_tables(counts[0, :N_EXPERTS].astype(I32), n_rows // TILE_ROWS)
    rows, w1lb = _dispatch(dest_flat, pad_first, n_pad, h2p, moe_w1[0], n_rows)
    e, f = N_EXPERTS, EXPERT_DIM
    out_rows = _experts(tables, rows, w1gb, w1lb, moe_b1[0].reshape(e, 1, 2 * f), w2b, moe_b2[0].reshape(e, 1, d))

    y = _combine(dest_flat, out_rows, x1, gates, mod3, seq, row(final_norm_g))
    return y.reshape(bsz, seq, d)
```
